```python
import math
import jax, jax.numpy as jnp
from jax import lax
import numpy as np

D_MODEL = 1024
BATCH = 8
SEQ = 2048
DEPTH = 1
DEC_BATCH = 128
DEC_SEQ = 8
PAST_LEN = 16384
PAGE_SIZE = 128

DN_HEADS = D_MODEL // 128
DN_DK = 128
DN_DV = 128
DN_KEY_WIDTH = DN_HEADS * DN_DK
DN_VAL_WIDTH = DN_HEADS * DN_DV
DN_QKV = 2 * DN_KEY_WIDTH + DN_VAL_WIDTH
CONV_W = 4
DN_CHUNK = 64
SGU_CHUNK = 128
SGU_GC = 128
SGU_GROUPS = D_MODEL // 256
SGU_WIDTH = SGU_GROUPS * SGU_GC
MEM_TOKENS = 256
MEM_HEADS = 4
MEM_HD = D_MODEL // 8
MEM_WIDTH = MEM_HEADS * MEM_HD
N_BRANCH = 3
D_FF = -(-8 * D_MODEL // (3 * 256)) * 256
IN_SPLITS = (DN_QKV, DN_VAL_WIDTH, DN_HEADS, DN_HEADS, SGU_WIDTH, SGU_WIDTH, MEM_WIDTH, N_BRANCH * D_MODEL)
D_IN = sum(IN_SPLITS)
RMS_EPS = 1e-6
L2_EPS = 1e-6

kernel_name = 'hybrid_deltanet_sgu_memory_step'


def rmsnorm(x, gain):
    xf = x.astype(jnp.float32)
    y = xf * lax.rsqrt(jnp.mean(xf * xf, axis=-1, keepdims=True) + RMS_EPS)
    return (y * gain.astype(jnp.float32)).astype(x.dtype)


def l2norm(x):
    return x * lax.rsqrt(jnp.sum(x * x, axis=-1, keepdims=True) + L2_EPS)


def short_conv(x_raw, conv0, conv_w):
    T = x_raw.shape[1]
    xc = jnp.concatenate([conv0.astype(x_raw.dtype), x_raw], axis=1)
    y = xc[:, 0:T] * conv_w[0]
    for i in range(1, CONV_W):
        y = y + xc[:, i:i + T] * conv_w[i]
    return jax.nn.silu(y), xc[:, xc.shape[1] - (CONV_W - 1):]


def gated_delta_rule(q, k, v, g, beta, s0):
    B, T, H, DK = q.shape
    DV = v.shape[-1]
    C = DN_CHUNK
    n = -(-T // C)
    pad = n * C - T

    def to_blocks(a):
        a = jnp.pad(a, [(0, 0), (0, pad)] + [(0, 0)] * (a.ndim - 2))
        a = a.reshape((B, n, C) + a.shape[2:])
        return jnp.moveaxis(a, 3, 1)

    q, k, v, g, beta = (to_blocks(a) for a in (q, k, v, g, beta))
    q = q * DK ** -0.5
    g = jnp.cumsum(g, axis=-1)
    causal = jnp.tril(jnp.ones((C, C), dtype=bool))
    strict = jnp.tril(jnp.ones((C, C), dtype=bool), -1)
    decay = jnp.exp(jnp.where(causal, g[..., :, None] - g[..., None, :], -jnp.inf))
    k_beta = k * beta[..., None]
    lmat = jnp.where(strict, jnp.einsum('bhnid,bhnjd->bhnij', k_beta, k) * decay, 0.0)
    eye = jnp.eye(C, dtype=jnp.float32)
    tmat = lax.linalg.triangular_solve(eye + lmat, jnp.broadcast_to(eye, lmat.shape),
                                       left_side=True, lower=True, unit_diagonal=True)
    u = jnp.einsum('bhnij,bhnje->bhnie', tmat, v * beta[..., None])
    w = jnp.einsum('bhnij,bhnjd->bhnid', tmat, k_beta * jnp.exp(g)[..., None])
    a_intra = jnp.where(causal, jnp.einsum('bhnid,bhnjd->bhnij', q, k) * decay, 0.0)
    q_dec = q * jnp.exp(g)[..., None]
    k_dec = k * jnp.exp(g[..., -1:] - g)[..., None]
    chunk_decay = jnp.exp(g[..., -1])

    def chunk_step(S, blk):
        q_i, k_i, u_i, w_i, a_i, dl = blk
        v_new = u_i - jnp.einsum('bhcd,bhde->bhce', w_i, S)
        o_i = jnp.einsum('bhcd,bhde->bhce', q_i, S) + jnp.einsum('bhcj,bhje->bhce', a_i, v_new)
        S = S * dl[..., None, None] + jnp.einsum('bhcd,bhce->bhde', k_i, v_new)
        return S, o_i

    xs = tuple(jnp.moveaxis(a, 2, 0) for a in (q_dec, k_dec, u, w, a_intra, chunk_decay))
    s_fin, o = lax.scan(chunk_step, s0, xs)
    o = jnp.transpose(o, (1, 0, 3, 2, 4)).reshape(B, n * C, H, DV)[:, :T]
    return o, s_fin


def spatial_gating(u, v, w_s, b_s):
    B, T, W = v.shape
    n = -(-T // SGU_CHUNK)
    pad = n * SGU_CHUNK - T
    vb = jnp.pad(v, ((0, 0), (0, pad), (0, 0))).reshape(B, n, SGU_CHUNK, SGU_GROUPS, SGU_GC)
    causal = jnp.tril(jnp.ones((SGU_CHUNK, SGU_CHUNK), dtype=bool))
    w_c = jnp.where(causal, w_s, 0.0)
    mixed = jnp.einsum('gts,bnsgc->bntgc', w_c, vb) + b_s.T[None, None, :, :, None]
    mixed = mixed.reshape(B, n * SGU_CHUNK, W)[:, :T]
    return u * mixed.astype(u.dtype)


def memory_kv(mem, gain, w_kv):
    B, M, _ = mem.shape
    kv = rmsnorm(mem, gain) @ w_kv
    mk, mv = jnp.split(kv, 2, axis=-1)
    return mk.reshape(B, M, MEM_HEADS, MEM_HD), mv.reshape(B, M, MEM_HEADS, MEM_HD)


def memory_attention(q, mem_k, mem_v):
    s = jnp.einsum('bthd,bmhd->bhtm', q.astype(jnp.float32), mem_k.astype(jnp.float32)) * MEM_HD ** -0.5
    p = jax.nn.softmax(s, axis=-1)
    return jnp.einsum('bhtm,bmhd->bthd', p, mem_v.astype(jnp.float32)).astype(q.dtype)


def block(x, mem_k, mem_v, s0, conv0, lw):
    (norm_mix, w_in, conv_w, a_log, dt_bias, dn_norm, sgu_norm, sgu_w, sgu_b,
     w_br_dn, w_br_sgu, w_br_mem, w_o, norm_ffn, w_gate_up, w_down) = lw
    f32 = jnp.float32
    B, T, _ = x.shape
    xn = rmsnorm(x, norm_mix)
    offsets = np.cumsum(IN_SPLITS)[:-1].tolist()
    qkv_raw, z, beta_raw, a_raw, u_raw, v_raw, q_mem, gate_raw = jnp.split(xn @ w_in, offsets, axis=-1)

    qkv, conv_new = short_conv(qkv_raw, conv0, conv_w)
    q, k, v = jnp.split(qkv.astype(f32), [DN_KEY_WIDTH, 2 * DN_KEY_WIDTH], axis=-1)
    q = l2norm(q.reshape(B, T, DN_HEADS, DN_DK))
    k = l2norm(k.reshape(B, T, DN_HEADS, DN_DK))
    v = v.reshape(B, T, DN_HEADS, DN_DV)
    beta = jax.nn.sigmoid(beta_raw.astype(f32))
    g = -jnp.exp(a_log.astype(f32)) * jax.nn.softplus(a_raw.astype(f32) + dt_bias.astype(f32))
    o_dn, s_new = gated_delta_rule(q, k, v, g, beta, s0.astype(f32))
    o_dn = rmsnorm(o_dn, dn_norm) * jax.nn.silu(z.astype(f32).reshape(B, T, DN_HEADS, DN_DV))
    o_dn = o_dn.reshape(B, T, DN_VAL_WIDTH).astype(x.dtype)

    u_sg = jax.nn.gelu(u_raw)
    v_sg = rmsnorm(jax.nn.gelu(v_raw), sgu_norm)
    o_sg = spatial_gating(u_sg, v_sg, sgu_w, sgu_b)

    o_mem = memory_attention(q_mem.reshape(B, T, MEM_HEADS, MEM_HD), mem_k, mem_v).reshape(B, T, MEM_WIDTH)

    gates = jax.nn.sigmoid(gate_raw).reshape(B, T, N_BRANCH, D_MODEL)
    merged = (gates[:, :, 0] * (o_dn @ w_br_dn)
              + gates[:, :, 1] * (o_sg @ w_br_sgu)
              + gates[:, :, 2] * (o_mem @ w_br_mem))
    x = x + merged @ w_o

    hg, hu = jnp.split(rmsnorm(x, norm_ffn) @ w_gate_up, 2, axis=-1)
    x = x + (jax.nn.silu(hg) * hu) @ w_down
    return x, s_new.astype(s0.dtype), conv_new, v_sg


def setup_inputs(seed: int = 0) -> dict:
    key = jax.random.key(seed)
    ks = jax.random.split(key, 26)
    f32 = jnp.float32
    L = DEPTH

    def nrm(k, shape, scale=1.0):
        return jax.random.normal(k, shape, f32) * scale

    def gain(k, shape):
        return 1.0 + 0.02 * jax.random.normal(k, shape, f32)

    dt = jnp.exp(jax.random.uniform(ks[11], (L, DN_HEADS), f32, math.log(1e-3), math.log(1e-1)))
    dt_bias = dt + jnp.log(-jnp.expm1(-dt))
    return {
        'x_prompt': nrm(ks[0], (BATCH, SEQ, D_MODEL)),
        'x_sample': nrm(ks[1], (DEC_BATCH, DEC_SEQ, D_MODEL)),
        'state_delta': nrm(ks[2], (L, DEC_BATCH, DN_HEADS, DN_DK, DN_DV), 0.5),
        'state_conv': nrm(ks[3], (L, DEC_BATCH, CONV_W - 1, DN_QKV)),
        'cache_mem_k': nrm(ks[4], (L, DEC_BATCH, MEM_TOKENS, MEM_HEADS, MEM_HD)),
        'cache_mem_v': nrm(ks[5], (L, DEC_BATCH, MEM_TOKENS, MEM_HEADS, MEM_HD)),
        'mem_prompt': nrm(ks[6], (BATCH, MEM_TOKENS, D_MODEL)),
        'norm_mix': gain(ks[7], (L, D_MODEL)),
        'w_in': nrm(ks[8], (L, D_MODEL, D_IN), D_MODEL ** -0.5),
        'conv_w': nrm(ks[9], (L, CONV_W, DN_QKV), CONV_W ** -0.5),
        'a_log': jnp.log(jax.random.uniform(ks[10], (L, DN_HEADS), f32, 1.0, 16.0)),
        'dt_bias': dt_bias,
        'dn_norm': gain(ks[12], (L, DN_DV)),
        'sgu_norm': gain(ks[13], (L, SGU_WIDTH)),
        'sgu_w': nrm(ks[14], (L, SGU_GROUPS, SGU_CHUNK, SGU_CHUNK), SGU_CHUNK ** -0.5),
        'sgu_b': gain(ks[15], (L, SGU_GROUPS, SGU_CHUNK)),
        'w_br_dn': nrm(ks[16], (L, DN_VAL_WIDTH, D_MODEL), DN_VAL_WIDTH ** -0.5),
        'w_br_sgu': nrm(ks[17], (L, SGU_WIDTH, D_MODEL), SGU_WIDTH ** -0.5),
        'w_br_mem': nrm(ks[18], (L, MEM_WIDTH, D_MODEL), MEM_WIDTH ** -0.5),
        'w_o': nrm(ks[19], (L, D_MODEL, D_MODEL), D_MODEL ** -0.5),
        'mem_norm': gain(ks[20], (L, D_MODEL)),
        'w_mem_kv': nrm(ks[21], (L, D_MODEL, 2 * MEM_WIDTH), D_MODEL ** -0.5),
        'norm_ffn': gain(ks[22], (L, D_MODEL)),
        'w_gate_up': nrm(ks[23], (L, D_MODEL, 2 * D_FF), D_MODEL ** -0.5),
        'w_down': nrm(ks[24], (L, D_FF, D_MODEL), D_FF ** -0.5),
        'norm_final': gain(ks[25], (D_MODEL,)),
    }


def reference(x_prompt, x_sample, state_delta, state_conv, cache_mem_k, cache_mem_v, mem_prompt,
              norm_mix, w_in, conv_w, a_log, dt_bias, dn_norm, sgu_norm, sgu_w, sgu_b,
              w_br_dn, w_br_sgu, w_br_mem, w_o, mem_norm, w_mem_kv, norm_ffn, w_gate_up, w_down, norm_final):
    xp, xs = x_prompt, x_sample
    bp = xp.shape[0]
    s0_p = jnp.zeros((bp, DN_HEADS, DN_DK, DN_DV), xp.dtype)
    c0_p = jnp.zeros((bp, CONV_W - 1, DN_QKV), xp.dtype)
    dn_p, cv_p, mk_p, mv_p, dn_s, cv_s, sg_s = [], [], [], [], [], [], []
    for l in range(DEPTH):
        lw = (norm_mix[l], w_in[l], conv_w[l], a_log[l], dt_bias[l], dn_norm[l], sgu_norm[l], sgu_w[l], sgu_b[l],
              w_br_dn[l], w_br_sgu[l], w_br_mem[l], w_o[l], norm_ffn[l], w_gate_up[l], w_down[l])
        mem_k, mem_v = memory_kv(mem_prompt, mem_norm[l], w_mem_kv[l])
        xp, s_p, c_p, _ = block(xp, mem_k, mem_v, s0_p, c0_p, lw)
        xs, s_s, c_s, v_s = block(xs, cache_mem_k[l], cache_mem_v[l], state_delta[l], state_conv[l], lw)
        dn_p.append(s_p); cv_p.append(c_p); mk_p.append(mem_k); mv_p.append(mem_v)
        dn_s.append(s_s); cv_s.append(c_s); sg_s.append(v_s)
    y_prompt = rmsnorm(xp, norm_final)
    y_sample = rmsnorm(xs, norm_final)
    return (y_prompt, y_sample, jnp.stack(dn_p), jnp.stack(cv_p), jnp.stack(mk_p), jnp.stack(mv_p),
            jnp.stack(dn_s), jnp.stack(cv_s), jnp.stack(sg_s))
```

```python
import functools
import math

import jax
import jax.numpy as jnp
from jax import lax
from jax.experimental import pallas as pl
from jax.experimental.pallas import tpu as pltpu

F32 = jnp.float32
BF16 = jnp.bfloat16

RMS_EPS = 1e-6
L2_EPS = 1e-6
CONV_W = 4
DN_CHUNK = 64
SGU_CHUNK = 128
LANES = 128
SUBLANES = 8
VMEM_LIMIT = 56 * 1024 * 1024


def _cparams(n_axes):
    return pltpu.CompilerParams(
        dimension_semantics=("arbitrary",) * n_axes, vmem_limit_bytes=VMEM_LIMIT)


def _mm(a, b):
    return jnp.dot(a.astype(BF16), b.astype(BF16), preferred_element_type=F32)


def _mm_nt(a, b):
    return lax.dot_general(a.astype(BF16), b.astype(BF16), (((1,), (1,)), ((), ())),
                           preferred_element_type=F32)


def _mm_tn(a, b):
    return lax.dot_general(a.astype(BF16), b.astype(BF16), (((0,), (0,)), ((), ())),
                           preferred_element_type=F32)


def _rms(x, gain):
    return x * lax.rsqrt(jnp.mean(x * x, axis=-1, keepdims=True) + RMS_EPS) * gain


def _sigmoid(x):
    return 1.0 / (1.0 + jnp.exp(-x))


def _silu(x):
    return x * _sigmoid(x)


def _softplus(x):
    return jnp.maximum(x, 0.0) + jnp.log1p(jnp.exp(-jnp.abs(x)))


def _const_spec(shape):
    nd = len(shape)
    return pl.BlockSpec(shape, lambda *_: (0,) * nd)


def _in_proj_kernel(x_ref, g_ref, w_ref, sgn_ref, qkv_ref, z_ref, ba_ref, u_ref, v_ref, qm_ref,
                    *, offs):
    xn = _rms(x_ref[...], g_ref[...]).astype(BF16)

    def proj(k):
        return jnp.dot(xn, w_ref[:, offs[k]:offs[k + 1]], preferred_element_type=F32)

    qkv_ref[...] = proj(0)
    z_ref[...] = proj(1)
    u_ref[...] = jax.nn.gelu(proj(2))
    v_ref[...] = _rms(jax.nn.gelu(proj(3)), sgn_ref[...])
    qm_ref[...] = proj(4)
    ba_ref[...] = proj(5)


def _in_proj(x2d, norm_mix, w1, sgu_norm, widths, tm):
    n, d = x2d.shape
    offs = (0,) + tuple(sum(widths[:k + 1]) for k in range(len(widths)))
    out_order = (0, 1, 5, 2, 3, 4)
    out_shape = tuple(jax.ShapeDtypeStruct((n, widths[k]), F32) for k in out_order)
    out_specs = tuple(pl.BlockSpec((tm, widths[k]), lambda i: (i, 0)) for k in out_order)
    return pl.pallas_call(
        functools.partial(_in_proj_kernel, offs=offs),
        grid=(n // tm,),
        in_specs=[pl.BlockSpec((tm, d), lambda i: (i, 0)),
                  _const_spec((1, d)),
                  _const_spec(w1.shape),
                  _const_spec((1, sgu_norm.shape[-1]))],
        out_specs=out_specs,
        out_shape=out_shape,
        compiler_params=_cparams(1),
        name="in_proj",
    )(x2d, norm_mix.reshape(1, d), w1, sgu_norm.reshape(1, -1))


def _log2(v):
    r = int(math.log2(v))
    assert 1 << r == v
    return r


def _iotas(n):
    ri = lax.broadcasted_iota(jnp.int32, (n, n), 0)
    ci = lax.broadcasted_iota(jnp.int32, (n, n), 1)
    return ri, ci


def _same_block(ri, ci, size):
    s = _log2(size)
    return (ri >> s) == (ci >> s)


def _unit_lower_inverse(lmat, ri, ci, blk, base):
    eye = (ri == ci).astype(F32)
    lb = jnp.where(_same_block(ri, ci, base), lmat, 0.0) if base < blk else lmat
    p = lb
    t = eye - lb
    k = 2
    while k < base:
        p = _mm(p, p)
        t = t + _mm(t, p)
        k *= 2
    s = base
    while s < blk:
        sh = _log2(s)
        off = _same_block(ri, ci, 2 * s) & (((ri >> sh) & 1) == 1) & (((ci >> sh) & 1) == 0)
        lo = jnp.where(off, lmat, 0.0)
        t = t - _mm(t, _mm(lo, t))
        s *= 2
    return t


def _chunk_cumsum(g, chunk):
    n = g.shape[0]
    row = lax.broadcasted_iota(jnp.int32, g.shape, 0) & (chunk - 1)
    s = 1
    while s < chunk:
        shifted = pltpu.roll(g, s, 0)
        g = g + jnp.where(row >= s, shifted, 0.0)
        s *= 2
    return g


def _delta_gates(ba, alog_row, dtb_row, chunk):
    beta = _sigmoid(ba)
    g = -jnp.exp(alog_row) * _softplus(ba + dtb_row)
    gc = _chunk_cumsum(g, chunk)
    return beta, gc, gc.T


def _head_prep(q, k, v, beta_h, gc_h, gcrow_h, glast_h, causal, strict, ri, ci, blk, base):
    dk = q.shape[-1]
    q = q * lax.rsqrt(jnp.sum(q * q, axis=-1, keepdims=True) + L2_EPS) * (dk ** -0.5)
    k = k * lax.rsqrt(jnp.sum(k * k, axis=-1, keepdims=True) + L2_EPS)
    eg = jnp.exp(gc_h)
    kb = k * beta_h
    vb = v * beta_h
    diff = gc_h - gcrow_h
    decay = jnp.exp(jnp.where(causal, diff, -jnp.inf))
    lmat = jnp.where(strict, _mm_nt(kb, k) * decay, 0.0)
    a_intra = _mm_nt(q, k) * decay
    tmat = _unit_lower_inverse(lmat, ri, ci, blk, base)
    uw = _mm(tmat, jnp.concatenate([vb, kb * eg], axis=1))
    u = uw[:, :v.shape[-1]]
    w = uw[:, v.shape[-1]:]
    q_dec = q * eg
    k_dec = k * jnp.exp(glast_h - gc_h)
    return u, w, a_intra, q_dec, k_dec


def _out_norm_gate(o, dn_gain, z):
    return (_rms(o, dn_gain) * _silu(z)).astype(BF16)


def _delta_prompt_kernel(qkv_ref, z_ref, ba_ref, cw_ref, alog_ref, dtb_ref, dng_ref,
                         o_ref, s_ref, xc_ref, vn_ref, *, tb, heads, dk):
    nb = pl.program_id(1)
    hd = SUBLANES

    @pl.when(nb == 0)
    def _():
        s_ref[...] = jnp.zeros_like(s_ref)
        xc_ref[0:hd, :] = jnp.zeros((hd, xc_ref.shape[1]), F32)

    xc_ref[hd:hd + tb, :] = qkv_ref[0]
    cw = cw_ref[...]
    y = xc_ref[pl.ds(hd - (CONV_W - 1), tb), :] * cw[0:1, :]
    for i in range(1, CONV_W):
        y = y + xc_ref[pl.ds(hd - (CONV_W - 1) + i, tb), :] * cw[i:i + 1, :]
    xc_ref[0:hd, :] = xc_ref[tb:tb + hd, :]
    qkv = _silu(y)

    beta, gc, gct = _delta_gates(ba_ref[0], alog_ref[...], dtb_ref[...], DN_CHUNK)
    nchunk = tb // DN_CHUNK
    glast = jnp.concatenate(
        [jnp.broadcast_to(gc[(c + 1) * DN_CHUNK - 1:(c + 1) * DN_CHUNK, :], (DN_CHUNK, gc.shape[1]))
         for c in range(nchunk)], axis=0)

    ri, ci = _iotas(tb)
    same = _same_block(ri, ci, DN_CHUNK)
    causal = same & (ri >= ci)
    strict = same & (ri > ci)
    kw = heads * dk
    z = z_ref[0]
    dng = dng_ref[...]

    for h in range(heads):
        q = qkv[:, h * dk:(h + 1) * dk]
        k = qkv[:, kw + h * dk:kw + (h + 1) * dk]
        v = qkv[:, 2 * kw + h * dk:2 * kw + (h + 1) * dk]
        gcol = heads + h
        u, w, a_intra, q_dec, k_dec = _head_prep(
            q, k, v, beta[:, h:h + 1], gc[:, gcol:gcol + 1], gct[gcol:gcol + 1, :],
            glast[:, gcol:gcol + 1], causal, strict, ri, ci, DN_CHUNK, 16)
        vn_ref[...] = jnp.zeros_like(vn_ref)
        outs = []
        for c in range(nchunk):
            r0, r1 = c * DN_CHUNK, (c + 1) * DN_CHUNK
            s = s_ref[0, h]
            wq = _mm(jnp.concatenate([w[r0:r1], q_dec[r0:r1]], axis=0), s)
            v_new = u[r0:r1] - wq[:DN_CHUNK]
            vn_ref[r0:r1, :] = v_new
            outs.append(wq[DN_CHUNK:] + _mm(a_intra[r0:r1, :], vn_ref[...]))
            dl = jnp.exp(glast[r0:r0 + 1, gcol:gcol + 1])
            s_ref[0, h] = s * dl + _mm_tn(k_dec[r0:r1], v_new)
        o = jnp.concatenate(outs, axis=0)
        o_ref[0, :, h * dk:(h + 1) * dk] = _out_norm_gate(o, dng, z[:, h * dk:(h + 1) * dk])


def _delta_prompt(qkv, z, ba, conv_w, alog_row, dtb_row, dn_norm, heads, dk, tb):
    b, t, c3 = qkv.shape
    vw = z.shape[-1]
    kern = functools.partial(_delta_prompt_kernel, tb=tb, heads=heads, dk=dk)
    return pl.pallas_call(
        kern,
        grid=(b, t // tb),
        in_specs=[pl.BlockSpec((1, tb, c3), lambda i, j: (i, j, 0)),
                  pl.BlockSpec((1, tb, vw), lambda i, j: (i, j, 0)),
                  pl.BlockSpec((1, tb, LANES), lambda i, j: (i, j, 0)),
                  _const_spec(conv_w.shape),
                  _const_spec((1, LANES)), _const_spec((1, LANES)), _const_spec((1, dk))],
        out_specs=(pl.BlockSpec((1, tb, vw), lambda i, j: (i, j, 0)),
                   pl.BlockSpec((1, heads, dk, dk), lambda i, j: (i, 0, 0, 0))),
        out_shape=(jax.ShapeDtypeStruct((b, t, vw), BF16),
                   jax.ShapeDtypeStruct((b, heads, dk, dk), F32)),
        scratch_shapes=[pltpu.VMEM((tb + 2 * SUBLANES, c3), F32),
                        pltpu.VMEM((tb, dk), F32)],
        compiler_params=_cparams(2),
        name="delta_prompt",
    )(qkv, z, ba, conv_w, alog_row, dtb_row, dn_norm.reshape(1, dk))


def _delta_sample_kernel(xq_ref, xk_ref, xv_ref, cwq_ref, cwk_ref, cwv_ref, z_ref, ba_ref,
                         alog_ref, dtb_ref, dng_ref, s0_ref, o_ref, s_ref, vn_ref, qs_ref,
                         *, nseq, t, heads):
    h = pl.program_id(1)
    rows = nseq * t
    grp = 2 * t

    def conv(x_ref, cw_ref):
        x = x_ref[...]
        cw = cw_ref[...]
        y = x * cw[CONV_W - 1:CONV_W, :]
        for i in range(CONV_W - 1):
            y = y + pltpu.roll(x, CONV_W - 1 - i, 0) * cw[i:i + 1, :]
        y = y.reshape(nseq, grp, y.shape[-1])[:, t:, :].reshape(rows, y.shape[-1])
        return _silu(y)

    q = conv(xq_ref, cwq_ref)
    k = conv(xk_ref, cwk_ref)
    v = conv(xv_ref, cwv_ref)

    beta, gc, gct = _delta_gates(ba_ref[...], alog_ref[...], dtb_ref[...], t)
    lane = lax.broadcasted_iota(jnp.int32, (rows, LANES), 1)
    beta_h = jnp.sum(jnp.where(lane == h, beta, 0.0), axis=1, keepdims=True)
    gc_h = jnp.sum(jnp.where(lane == heads + h, gc, 0.0), axis=1, keepdims=True)
    sub = lax.broadcasted_iota(jnp.int32, (LANES, rows), 0)
    gcrow_h = jnp.sum(jnp.where(sub == heads + h, gct, 0.0), axis=0, keepdims=True)
    glast_h = jnp.concatenate(
        [jnp.broadcast_to(gc_h[(b + 1) * t - 1:(b + 1) * t, :], (t, 1)) for b in range(nseq)], axis=0)

    ri, ci = _iotas(rows)
    same = _same_block(ri, ci, t)
    causal = same & (ri >= ci)
    strict = same & (ri > ci)
    u, w, a_intra, q_dec, k_dec = _head_prep(
        q, k, v, beta_h, gc_h, gcrow_h, glast_h, causal, strict, ri, ci, t, t)

    for b in range(nseq):
        r0, r1 = b * t, (b + 1) * t
        wq = _mm(jnp.concatenate([w[r0:r1], q_dec[r0:r1]], axis=0), s0_ref[b, 0])
        vn_ref[r0:r1, :] = u[r0:r1] - wq[:t]
        qs_ref[r0:r1, :] = wq[t:]
    v_new = vn_ref[...]
    o = qs_ref[...] + _mm(a_intra, v_new)
    o_ref[...] = _out_norm_gate(o, dng_ref[...], z_ref[...])

    kt = k_dec.T
    col_seq = lax.broadcasted_iota(jnp.int32, kt.shape, 1) >> _log2(t)
    v_new_b = v_new.astype(BF16)
    for b in range(nseq):
        dl = jnp.exp(glast_h[b * t:b * t + 1, :])
        upd = jnp.dot(jnp.where(col_seq == b, kt, 0.0).astype(BF16), v_new_b,
                      preferred_element_type=F32)
        s_ref[b, 0] = s0_ref[b, 0] * dl + upd


def _delta_sample(xc, z, ba, conv_w, alog_row, dtb_row, dn_norm, s0, heads, dk, t, nseq):
    rows_all = z.shape[0]
    b = rows_all // t
    kern = functools.partial(_delta_sample_kernel, nseq=nseq, t=t, heads=heads)
    xblk = (nseq * 2 * t, dk)
    cwblk = (CONV_W, dk)
    return pl.pallas_call(
        kern,
        grid=(b // nseq, heads),
        in_specs=[pl.BlockSpec(xblk, lambda i, h: (i, h)),
                  pl.BlockSpec(xblk, lambda i, h: (i, heads + h)),
                  pl.BlockSpec(xblk, lambda i, h: (i, 2 * heads + h)),
                  pl.BlockSpec(cwblk, lambda i, h: (0, h)),
                  pl.BlockSpec(cwblk, lambda i, h: (0, heads + h)),
                  pl.BlockSpec(cwblk, lambda i, h: (0, 2 * heads + h)),
                  pl.BlockSpec((nseq * t, dk), lambda i, h: (i, h)),
                  pl.BlockSpec((nseq * t, LANES), lambda i, h: (i, 0)),
                  _const_spec((1, LANES)), _const_spec((1, LANES)), _const_spec((1, dk)),
                  pl.BlockSpec((nseq, 1, dk, dk), lambda i, h: (i, h, 0, 0))],
        out_specs=(pl.BlockSpec((nseq * t, dk), lambda i, h: (i, h)),
                   pl.BlockSpec((nseq, 1, dk, dk), lambda i, h: (i, h, 0, 0))),
        out_shape=(jax.ShapeDtypeStruct((rows_all, heads * dk), BF16),
                   jax.ShapeDtypeStruct(s0.shape, F32)),
        scratch_shapes=[pltpu.VMEM((nseq * t, dk), F32), pltpu.VMEM((nseq * t, dk), F32)],
        compiler_params=_cparams(2),
        name="delta_sample",
    )(xc, xc, xc, conv_w, conv_w, conv_w, z, ba, alog_row, dtb_row, dn_norm.reshape(1, dk), s0)


def _sgu_kernel(u_ref, v_ref, w_ref, b_ref, o_ref, *, groups, gc, blk):
    rows = u_ref.shape[0]
    ri, ci = _iotas(rows)
    causal = _same_block(ri, ci, blk) & (ri >= ci)
    bias = b_ref[...]
    for g in range(groups):
        wc = jnp.where(causal, w_ref[g], 0.0)
        mixed = _mm(wc, v_ref[:, g * gc:(g + 1) * gc]) + bias[:, g:g + 1]
        o_ref[:, g * gc:(g + 1) * gc] = (u_ref[:, g * gc:(g + 1) * gc] * mixed).astype(BF16)


def _sgu(u, v, w_tiles, b_cols, blk):
    n, width = u.shape
    groups, rows, _ = w_tiles.shape
    kern = functools.partial(_sgu_kernel, groups=groups, gc=width // groups, blk=blk)
    return pl.pallas_call(
        kern,
        grid=(n // rows,),
        in_specs=[pl.BlockSpec((rows, width), lambda i: (i, 0)),
                  pl.BlockSpec((rows, width), lambda i: (i, 0)),
                  _const_spec(w_tiles.shape), _const_spec(b_cols.shape)],
        out_specs=pl.BlockSpec((rows, width), lambda i: (i, 0)),
        out_shape=jax.ShapeDtypeStruct((n, width), BF16),
        compiler_params=_cparams(1),
        name="sgu",
    )(u, v, w_tiles, b_cols)


def _mem_kv_kernel(m_ref, g_ref, w_ref, o_ref):
    o_ref[...] = _mm(_rms(m_ref[...], g_ref[...]), w_ref[...])


def _mem_kv(mem2d, gain, w_kv, tm):
    n, d = mem2d.shape
    return pl.pallas_call(
        _mem_kv_kernel,
        grid=(n // tm,),
        in_specs=[pl.BlockSpec((tm, d), lambda i: (i, 0)), _const_spec((1, d)),
                  _const_spec(w_kv.shape)],
        out_specs=pl.BlockSpec((tm, w_kv.shape[1]), lambda i: (i, 0)),
        out_shape=jax.ShapeDtypeStruct((n, w_kv.shape[1]), F32),
        compiler_params=_cparams(1),
        name="mem_kv",
    )(mem2d, gain.reshape(1, d), w_kv)


def _attend(q, k, v, scale):
    s = _mm_nt(q, k) * scale
    e = jnp.exp(s - jnp.max(s, axis=-1, keepdims=True))
    return _mm(e, v) / jnp.sum(e, axis=-1, keepdims=True)


def _mem_attn_prompt_kernel(q_ref, k_ref, v_ref, o_ref, *, heads, hd):
    scale = hd ** -0.5
    for h in range(heads):
        sl = slice(h * hd, (h + 1) * hd)
        o_ref[0, :, sl] = _attend(q_ref[0, :, sl], k_ref[0, :, sl], v_ref[0, :, sl], scale).astype(BF16)


def _mem_attn_prompt(q, mk, mv, heads, tq):
    b, t, w = q.shape
    m = mk.shape[1]
    kern = functools.partial(_mem_attn_prompt_kernel, heads=heads, hd=w // heads)
    return pl.pallas_call(
        kern,
        grid=(b, t // tq),
        in_specs=[pl.BlockSpec((1, tq, w), lambda i, j: (i, j, 0)),
                  pl.BlockSpec((1, m, w), lambda i, j: (i, 0, 0)),
                  pl.BlockSpec((1, m, w), lambda i, j: (i, 0, 0))],
        out_specs=pl.BlockSpec((1, tq, w), lambda i, j: (i, j, 0)),
        out_shape=jax.ShapeDtypeStruct((b, t, w), BF16),
        compiler_params=_cparams(2),
        name="mem_attn_prompt",
    )(q, mk, mv)


def _mem_attn_sample_kernel(q_ref, k_ref, v_ref, o_ref, *, nseq, t, heads, hd):
    scale = hd ** -0.5
    for b in range(nseq):
        for h in range(heads):
            sl = slice(h * hd, (h + 1) * hd)
            o = _attend(q_ref[b * t:(b + 1) * t, sl], k_ref[b, :, sl], v_ref[b, :, sl], scale)
            o_ref[b * t:(b + 1) * t, sl] = o


def _mem_attn_sample(q2d, mk, mv, heads, t, nseq):
    n, w = q2d.shape
    b, m, _ = mk.shape
    kern = functools.partial(_mem_attn_sample_kernel, nseq=nseq, t=t, heads=heads, hd=w // heads)
    return pl.pallas_call(
        kern,
        grid=(b // nseq,),
        in_specs=[pl.BlockSpec((nseq * t, w), lambda i: (i, 0)),
                  pl.BlockSpec((nseq, m, w), lambda i: (i, 0, 0)),
                  pl.BlockSpec((nseq, m, w), lambda i: (i, 0, 0))],
        out_specs=pl.BlockSpec((nseq * t, w), lambda i: (i, 0)),
        out_shape=jax.ShapeDtypeStruct((n, w), F32),
        compiler_params=_cparams(1),
        name="mem_attn_sample",
    )(q2d, mk, mv)


def _merge_kernel(x_ref, odn_ref, osg_ref, omem_ref, g_ref, wg_ref, wdn_ref, wsg_ref, wmem_ref,
                  wo_ref, o_ref):
    x = x_ref[...]
    d = x.shape[-1]
    xn = _rms(x, g_ref[...]).astype(BF16)
    merged = None
    for idx, (b_ref, w_ref) in enumerate(((odn_ref, wdn_ref), (osg_ref, wsg_ref), (omem_ref, wmem_ref))):
        gate = _sigmoid(jnp.dot(xn, wg_ref[:, idx * d:(idx + 1) * d], preferred_element_type=F32))
        term = gate * _mm(b_ref[...], w_ref[...])
        merged = term if merged is None else merged + term
    o_ref[...] = x + _mm(merged, wo_ref[...])


def _merge(x2d, odn, osg, omem, norm_mix, wg, wdn, wsg, wmem, wo, tm):
    n, d = x2d.shape
    row = lambda a: pl.BlockSpec((tm, a.shape[1]), lambda i: (i, 0))
    return pl.pallas_call(
        _merge_kernel,
        grid=(n // tm,),
        in_specs=[row(x2d), row(odn), row(osg), row(omem), _const_spec((1, d)),
                  _const_spec(wg.shape), _const_spec(wdn.shape), _const_spec(wsg.shape),
                  _const_spec(wmem.shape), _const_spec(wo.shape)],
        out_specs=row(x2d),
        out_shape=jax.ShapeDtypeStruct((n, d), F32),
        compiler_params=_cparams(1),
        name="merge",
    )(x2d, odn, osg, omem, norm_mix.reshape(1, d), wg, wdn, wsg, wmem, wo)


def _ffn_kernel(x_ref, gf_ref, wgu_ref, wd_ref, gl_ref, o_ref):
    x = x_ref[...]
    dff = wd_ref.shape[0]
    hn = _rms(x, gf_ref[...]).astype(BF16)
    hg = jnp.dot(hn, wgu_ref[:, :dff], preferred_element_type=F32)
    hu = jnp.dot(hn, wgu_ref[:, dff:], preferred_element_type=F32)
    x2 = x + _mm(_silu(hg) * hu, wd_ref[...])
    o_ref[...] = _rms(x2, gl_ref[...])


def _ffn(x2d, norm_ffn, wgu, wd, norm_final, tm):
    n, d = x2d.shape
    return pl.pallas_call(
        _ffn_kernel,
        grid=(n // tm,),
        in_specs=[pl.BlockSpec((tm, d), lambda i: (i, 0)), _const_spec((1, d)),
                  _const_spec(wgu.shape), _const_spec(wd.shape), _const_spec((1, d))],
        out_specs=pl.BlockSpec((tm, d), lambda i: (i, 0)),
        out_shape=jax.ShapeDtypeStruct((n, d), F32),
        compiler_params=_cparams(1),
        name="ffn",
    )(x2d, norm_ffn.reshape(1, d), wgu, wd, norm_final.reshape(1, d))


def kernel(x_prompt, x_sample, state_delta, state_conv, cache_mem_k, cache_mem_v, mem_prompt,
           norm_mix, w_in, conv_w, a_log, dt_bias, dn_norm, sgu_norm, sgu_w, sgu_b,
           w_br_dn, w_br_sgu, w_br_mem, w_o, mem_norm, w_mem_kv, norm_ffn, w_gate_up, w_down,
           norm_final):
    depth = w_in.shape[0]
    assert depth == 1, "single-layer stack only"
    bp, tp, d = x_prompt.shape
    bs, ts, _ = x_sample.shape
    heads, dk, dv = state_delta.shape[2:]
    assert dk == dv == LANES and 2 * heads <= LANES and ts == SUBLANES
    kw = heads * dk
    qkv_w = 2 * kw + heads * dv
    sgu_groups, sgu_chunk, _ = sgu_w.shape[1:]
    sgu_width = sgu_norm.shape[-1]
    mem_tokens, mem_heads, mem_hd = cache_mem_k.shape[2:]
    mem_w = mem_heads * mem_hd
    l = 0

    splits = (qkv_w, heads * dv, heads, heads, sgu_width, sgu_width, mem_w, 3 * d)
    o = [0]
    for s in splits:
        o.append(o[-1] + s)
    w = w_in[l]
    ba_pad = jnp.zeros((d, LANES - 2 * heads), w.dtype)
    w1 = jnp.concatenate([w[:, o[0]:o[2]], w[:, o[4]:o[7]], w[:, o[2]:o[4]], ba_pad], axis=1).astype(BF16)
    widths = (qkv_w, heads * dv, sgu_width, sgu_width, mem_w, LANES)
    w_gate = w[:, o[7]:o[8]].astype(BF16)

    lane_pad = lambda vec: jnp.zeros((1, LANES), F32).at[0, heads:2 * heads].set(vec)
    alog_row = lane_pad(a_log[l])
    dtb_row = lane_pad(dt_bias[l])

    wdn = w_br_dn[l].astype(BF16)
    wsg = w_br_sgu[l].astype(BF16)
    wmem = w_br_mem[l].astype(BF16)
    wo = w_o[l].astype(BF16)
    wgu = w_gate_up[l].astype(BF16)
    wd = w_down[l].astype(BF16)
    wkv = w_mem_kv[l].astype(BF16)

    tm = 256

    np_ = bp * tp
    xp2 = x_prompt.reshape(np_, d)
    qkv_p, z_p, ba_p, u_p, v_p, qm_p = _in_proj(xp2, norm_mix[l], w1, sgu_norm[l], widths, tm)
    odn_p, s_p = _delta_prompt(qkv_p.reshape(bp, tp, qkv_w), z_p.reshape(bp, tp, -1),
                               ba_p.reshape(bp, tp, LANES), conv_w[l], alog_row, dtb_row,
                               dn_norm[l], heads, dk, 256)
    conv_p = qkv_p.reshape(bp, tp, qkv_w)[:, tp - (CONV_W - 1):, :]
    osg_p = _sgu(u_p, v_p, sgu_w[l], sgu_b[l].T, SGU_CHUNK)
    kv_p = _mem_kv(mem_prompt.reshape(bp * mem_tokens, d), mem_norm[l], wkv, tm)
    mk_p = kv_p[:, :mem_w].reshape(bp, mem_tokens, mem_w)
    mv_p = kv_p[:, mem_w:].reshape(bp, mem_tokens, mem_w)
    omem_p = _mem_attn_prompt(qm_p.reshape(bp, tp, mem_w), mk_p, mv_p, mem_heads, 512)
    x1_p = _merge(xp2, odn_p.reshape(np_, -1), osg_p, omem_p.reshape(np_, mem_w), norm_mix[l],
                  w_gate, wdn, wsg, wmem, wo, tm)
    y_p = _ffn(x1_p, norm_ffn[l], wgu, wd, norm_final, tm)

    ns = bs * ts
    xs2 = x_sample.reshape(ns, d)
    qkv_s, z_s, ba_s, u_s, v_s, qm_s = _in_proj(xs2, norm_mix[l], w1, sgu_norm[l], widths, tm)
    qkv_s3 = qkv_s.reshape(bs, ts, qkv_w)
    hist = jnp.concatenate(
        [jnp.zeros((bs, ts - (CONV_W - 1), qkv_w), F32), state_conv[l]], axis=1)
    xc_s = jnp.concatenate([hist, qkv_s3], axis=1).reshape(bs * 2 * ts, qkv_w)
    nseq = LANES // ts
    odn_s, s_s = _delta_sample(xc_s, z_s, ba_s, conv_w[l], alog_row, dtb_row, dn_norm[l],
                               state_delta[l], heads, dk, ts, nseq)
    conv_s = qkv_s3[:, ts - (CONV_W - 1):, :]
    reps = SGU_CHUNK // ts
    w_tiles = jnp.tile(sgu_w[l][:, :ts, :ts], (1, reps, reps))
    b_cols = jnp.tile(sgu_b[l][:, :ts], (1, reps)).T
    osg_s = _sgu(u_s, v_s, w_tiles, b_cols, ts)
    omem_s = _mem_attn_sample(qm_s, cache_mem_k[l].reshape(bs, mem_tokens, mem_w),
                              cache_mem_v[l].reshape(bs, mem_tokens, mem_w), mem_heads, ts, 8)
    x1_s = _merge(xs2, odn_s, osg_s, omem_s, norm_mix[l], w_gate, wdn, wsg, wmem, wo, tm)
    y_s = _ffn(x1_s, norm_ffn[l], wgu, wd, norm_final, tm)

    return (y_p.reshape(bp, tp, d), y_s.reshape(bs, ts, d),
            s_p[None], conv_p[None],
            mk_p.reshape(1, bp, mem_tokens, mem_heads, mem_hd),
            mv_p.reshape(1, bp, mem_tokens, mem_heads, mem_hd),
            s_s[None], conv_s[None], v_s.reshape(1, bs, ts, sgu_width))
```

```python
import functools
import math

import jax
import jax.numpy as jnp
from jax import lax
from jax.experimental import pallas as pl
from jax.experimental.pallas import tpu as pltpu

F32 = jnp.float32
BF16 = jnp.bfloat16

RMS_EPS = 1e-6
L2_EPS = 1e-6
CONV_W = 4
DN_CHUNK = 64
SGU_CHUNK = 128
LANES = 128
SUBLANES = 8
VMEM_LIMIT = 56 * 1024 * 1024


def _cparams(n_axes):
    return pltpu.CompilerParams(
        dimension_semantics=("arbitrary",) * n_axes, vmem_limit_bytes=VMEM_LIMIT)


def _mm(a, b):
    return jnp.dot(a.astype(BF16), b.astype(BF16), preferred_element_type=F32)


def _mm_nt(a, b):
    return lax.dot_general(a.astype(BF16), b.astype(BF16), (((1,), (1,)), ((), ())),
                           preferred_element_type=F32)


def _mm_tn(a, b):
    return lax.dot_general(a.astype(BF16), b.astype(BF16), (((0,), (0,)), ((), ())),
                           preferred_element_type=F32)


def _rms(x, gain):
    return x * lax.rsqrt(jnp.mean(x * x, axis=-1, keepdims=True) + RMS_EPS) * gain


def _sigmoid(x):
    return 1.0 / (1.0 + jnp.exp(-x))


def _silu(x):
    return x * _sigmoid(x)


def _softplus(x):
    return jnp.maximum(x, 0.0) + jnp.log1p(jnp.exp(-jnp.abs(x)))


def _const_spec(shape):
    nd = len(shape)
    return pl.BlockSpec(shape, lambda *_: (0,) * nd)


def _in_proj_kernel(x_ref, g_ref, w_ref, sgn_ref, qkv_ref, z_ref, ba_ref, u_ref, v_ref, qm_ref,
                    *, offs):
    xn = _rms(x_ref[...], g_ref[...]).astype(BF16)

    def proj(k):
        return jnp.dot(xn, w_ref[:, offs[k]:offs[k + 1]], preferred_element_type=F32)

    qkv_ref[...] = proj(0)
    z_ref[...] = proj(1)
    u_ref[...] = jax.nn.gelu(proj(2))
    v_ref[...] = _rms(jax.nn.gelu(proj(3)), sgn_ref[...])
    qm_ref[...] = proj(4)
    ba_ref[...] = proj(5)


def _in_proj(x2d, norm_mix, w1, sgu_norm, widths, tm):
    n, d = x2d.shape
    offs = (0,) + tuple(sum(widths[:k + 1]) for k in range(len(widths)))
    out_order = (0, 1, 5, 2, 3, 4)
    out_shape = tuple(jax.ShapeDtypeStruct((n, widths[k]), F32) for k in out_order)
    out_specs = tuple(pl.BlockSpec((tm, widths[k]), lambda i: (i, 0)) for k in out_order)
    return pl.pallas_call(
        functools.partial(_in_proj_kernel, offs=offs),
        grid=(n // tm,),
        in_specs=[pl.BlockSpec((tm, d), lambda i: (i, 0)),
                  _const_spec((1, d)),
                  _const_spec(w1.shape),
                  _const_spec((1, sgu_norm.shape[-1]))],
        out_specs=out_specs,
        out_shape=out_shape,
        compiler_params=_cparams(1),
        name="in_proj",
    )(x2d, norm_mix.reshape(1, d), w1, sgu_norm.reshape(1, -1))


def _log2(v):
    r = int(math.log2(v))
    assert 1 << r == v
    return r


def _iotas(n):
    ri = lax.broadcasted_iota(jnp.int32, (n, n), 0)
    ci = lax.broadcasted_iota(jnp.int32, (n, n), 1)
    return ri, ci


def _same_block(ri, ci, size):
    s = _log2(size)
    return (ri >> s) == (ci >> s)


def _unit_lower_inverse(lmats, ri, ci, blk, base):
    eye = (ri == ci).astype(F32)
    if base < blk:
        same = _same_block(ri, ci, base)
        lbs = [jnp.where(same, lm, 0.0) for lm in lmats]
    else:
        lbs = list(lmats)
    ps = lbs
    ts = [eye - lb for lb in lbs]
    k = 2
    while k < base:
        ps = [_mm(p, p) for p in ps]
        ts = [t + _mm(t, p) for t, p in zip(ts, ps)]
        k *= 2
    s = base
    while s < blk:
        sh = _log2(s)
        off = _same_block(ri, ci, 2 * s) & (((ri >> sh) & 1) == 1) & (((ci >> sh) & 1) == 0)
        xs = [_mm(jnp.where(off, lm, 0.0), t) for lm, t in zip(lmats, ts)]
        ts = [t - _mm(t, x) for t, x in zip(ts, xs)]
        s *= 2
    return ts


def _chunk_cumsum(g, chunk):
    n = g.shape[0]
    row = lax.broadcasted_iota(jnp.int32, g.shape, 0) & (chunk - 1)
    s = 1
    while s < chunk:
        shifted = pltpu.roll(g, s, 0)
        g = g + jnp.where(row >= s, shifted, 0.0)
        s *= 2
    return g


def _delta_gates(ba, alog_row, dtb_row, chunk):
    beta = _sigmoid(ba)
    g = -jnp.exp(alog_row) * _softplus(ba + dtb_row)
    gc = _chunk_cumsum(g, chunk)
    return beta, gc, gc.T


def _heads_prep(qs, ks, vs, betas, gcs, gcrows, glasts, causal, strict, ri, ci, blk, base):
    dk = qs[0].shape[-1]
    dv = vs[0].shape[-1]
    qs = [q * lax.rsqrt(jnp.sum(q * q, axis=-1, keepdims=True) + L2_EPS) * (dk ** -0.5) for q in qs]
    ks = [k * lax.rsqrt(jnp.sum(k * k, axis=-1, keepdims=True) + L2_EPS) for k in ks]
    egs = [jnp.exp(gc) for gc in gcs]
    kbs = [k * b for k, b in zip(ks, betas)]
    vbs = [v * b for v, b in zip(vs, betas)]
    decays = [jnp.exp(jnp.where(causal, gc - gr, -jnp.inf)) for gc, gr in zip(gcs, gcrows)]
    lmats = [jnp.where(strict, _mm_nt(kb, k) * dec, 0.0) for kb, k, dec in zip(kbs, ks, decays)]
    a_intras = [_mm_nt(q, k) * dec for q, k, dec in zip(qs, ks, decays)]
    tmats = _unit_lower_inverse(lmats, ri, ci, blk, base)
    uws = [_mm(t, jnp.concatenate([vb, kb * eg], axis=1))
           for t, vb, kb, eg in zip(tmats, vbs, kbs, egs)]
    us = [uw[:, :dv] for uw in uws]
    ws = [uw[:, dv:] for uw in uws]
    q_decs = [q * eg for q, eg in zip(qs, egs)]
    k_decs = [k * jnp.exp(gl - gc) for k, gl, gc in zip(ks, glasts, gcs)]
    return us, ws, a_intras, q_decs, k_decs


def _out_norm_gate(o, dn_gain, z):
    return (_rms(o, dn_gain) * _silu(z)).astype(BF16)


def _delta_prompt_kernel(qkv_ref, z_ref, ba_ref, cw_ref, alog_ref, dtb_ref, dng_ref,
                         o_ref, s_ref, xc_ref, vn_ref, *, tb, heads, dk):
    nb = pl.program_id(1)
    hd = SUBLANES

    @pl.when(nb == 0)
    def _():
        s_ref[...] = jnp.zeros_like(s_ref)
        xc_ref[0:hd, :] = jnp.zeros((hd, xc_ref.shape[1]), F32)

    xc_ref[hd:hd + tb, :] = qkv_ref[0]
    cw = cw_ref[...]
    y = xc_ref[pl.ds(hd - (CONV_W - 1), tb), :] * cw[0:1, :]
    for i in range(1, CONV_W):
        y = y + xc_ref[pl.ds(hd - (CONV_W - 1) + i, tb), :] * cw[i:i + 1, :]
    xc_ref[0:hd, :] = xc_ref[tb:tb + hd, :]
    qkv = _silu(y)

    beta, gc, gct = _delta_gates(ba_ref[0], alog_ref[...], dtb_ref[...], DN_CHUNK)
    nchunk = tb // DN_CHUNK
    glast = jnp.concatenate(
        [jnp.broadcast_to(gc[(c + 1) * DN_CHUNK - 1:(c + 1) * DN_CHUNK, :], (DN_CHUNK, gc.shape[1]))
         for c in range(nchunk)], axis=0)

    ri, ci = _iotas(tb)
    same = _same_block(ri, ci, DN_CHUNK)
    causal = same & (ri >= ci)
    strict = same & (ri > ci)
    kw = heads * dk
    z = z_ref[0]
    dng = dng_ref[...]

    hs = range(heads)
    gcols = [heads + h for h in hs]
    us, ws, a_intras, q_decs, k_decs = _heads_prep(
        [qkv[:, h * dk:(h + 1) * dk] for h in hs],
        [qkv[:, kw + h * dk:kw + (h + 1) * dk] for h in hs],
        [qkv[:, 2 * kw + h * dk:2 * kw + (h + 1) * dk] for h in hs],
        [beta[:, h:h + 1] for h in hs],
        [gc[:, g:g + 1] for g in gcols],
        [gct[g:g + 1, :] for g in gcols],
        [glast[:, g:g + 1] for g in gcols],
        causal, strict, ri, ci, DN_CHUNK, 16)
    vn_ref[...] = jnp.zeros_like(vn_ref)
    outs = [[] for _ in hs]
    for c in range(nchunk):
        r0, r1 = c * DN_CHUNK, (c + 1) * DN_CHUNK
        for h in hs:
            s = s_ref[0, h]
            wq = _mm(jnp.concatenate([ws[h][r0:r1], q_decs[h][r0:r1]], axis=0), s)
            v_new = us[h][r0:r1] - wq[:DN_CHUNK]
            vn_ref[h, r0:r1, :] = v_new
            outs[h].append(wq[DN_CHUNK:] + _mm(a_intras[h][r0:r1, :], vn_ref[h]))
            dl = jnp.exp(glast[r0:r0 + 1, gcols[h]:gcols[h] + 1])
            s_ref[0, h] = s * dl + _mm_tn(k_decs[h][r0:r1], v_new)
    for h in hs:
        o = jnp.concatenate(outs[h], axis=0)
        o_ref[0, :, h * dk:(h + 1) * dk] = _out_norm_gate(o, dng, z[:, h * dk:(h + 1) * dk])


def _delta_prompt(qkv, z, ba, conv_w, alog_row, dtb_row, dn_norm, heads, dk, tb):
    b, t, c3 = qkv.shape
    vw = z.shape[-1]
    kern = functools.partial(_delta_prompt_kernel, tb=tb, heads=heads, dk=dk)
    return pl.pallas_call(
        kern,
        grid=(b, t // tb),
        in_specs=[pl.BlockSpec((1, tb, c3), lambda i, j: (i, j, 0)),
                  pl.BlockSpec((1, tb, vw), lambda i, j: (i, j, 0)),
                  pl.BlockSpec((1, tb, LANES), lambda i, j: (i, j, 0)),
                  _const_spec(conv_w.shape),
                  _const_spec((1, LANES)), _const_spec((1, LANES)), _const_spec((1, dk))],
        out_specs=(pl.BlockSpec((1, tb, vw), lambda i, j: (i, j, 0)),
                   pl.BlockSpec((1, heads, dk, dk), lambda i, j: (i, 0, 0, 0))),
        out_shape=(jax.ShapeDtypeStruct((b, t, vw), BF16),
                   jax.ShapeDtypeStruct((b, heads, dk, dk), F32)),
        scratch_shapes=[pltpu.VMEM((tb + 2 * SUBLANES, c3), F32),
                        pltpu.VMEM((heads, tb, dk), F32)],
        compiler_params=_cparams(2),
        name="delta_prompt",
    )(qkv, z, ba, conv_w, alog_row, dtb_row, dn_norm.reshape(1, dk))


def _delta_sample_kernel(xq_ref, xk_ref, xv_ref, cwq_ref, cwk_ref, cwv_ref, z_ref, ba_ref,
                         alog_ref, dtb_ref, dng_ref, s0_ref, o_ref, s_ref, vn_ref, qs_ref,
                         *, nseq, t, heads):
    h = pl.program_id(1)
    rows = nseq * t
    grp = 2 * t

    def conv(x_ref, cw_ref):
        x = x_ref[...]
        cw = cw_ref[...]
        y = x * cw[CONV_W - 1:CONV_W, :]
        for i in range(CONV_W - 1):
            y = y + pltpu.roll(x, CONV_W - 1 - i, 0) * cw[i:i + 1, :]
        y = y.reshape(nseq, grp, y.shape[-1])[:, t:, :].reshape(rows, y.shape[-1])
        return _silu(y)

    q = conv(xq_ref, cwq_ref)
    k = conv(xk_ref, cwk_ref)
    v = conv(xv_ref, cwv_ref)

    beta, gc, gct = _delta_gates(ba_ref[...], alog_ref[...], dtb_ref[...], t)
    lane = lax.broadcasted_iota(jnp.int32, (rows, LANES), 1)
    beta_h = jnp.sum(jnp.where(lane == h, beta, 0.0), axis=1, keepdims=True)
    gc_h = jnp.sum(jnp.where(lane == heads + h, gc, 0.0), axis=1, keepdims=True)
    sub = lax.broadcasted_iota(jnp.int32, (LANES, rows), 0)
    gcrow_h = jnp.sum(jnp.where(sub == heads + h, gct, 0.0), axis=0, keepdims=True)
    glast_h = jnp.concatenate(
        [jnp.broadcast_to(gc_h[(b + 1) * t - 1:(b + 1) * t, :], (t, 1)) for b in range(nseq)], axis=0)

    ri, ci = _iotas(rows)
    same = _same_block(ri, ci, t)
    causal = same & (ri >= ci)
    strict = same & (ri > ci)
    (u,), (w,), (a_intra,), (q_dec,), (k_dec,) = _heads_prep(
        [q], [k], [v], [beta_h], [gc_h], [gcrow_h], [glast_h], causal, strict, ri, ci, t, t)

    for b in range(nseq):
        r0, r1 = b * t, (b + 1) * t
        wq = _mm(jnp.concatenate([w[r0:r1], q_dec[r0:r1]], axis=0), s0_ref[b, 0])
        vn_ref[r0:r1, :] = u[r0:r1] - wq[:t]
        qs_ref[r0:r1, :] = wq[t:]
    v_new = vn_ref[...]
    o = qs_ref[...] + _mm(a_intra, v_new)
    o_ref[...] = _out_norm_gate(o, dng_ref[...], z_ref[...])

    kt = k_dec.T
    col_seq = lax.broadcasted_iota(jnp.int32, kt.shape, 1) >> _log2(t)
    v_new_b = v_new.astype(BF16)
    for b in range(nseq):
        dl = jnp.exp(glast_h[b * t:b * t + 1, :])
        upd = jnp.dot(jnp.where(col_seq == b, kt, 0.0).astype(BF16), v_new_b,
                      preferred_element_type=F32)
        s_ref[b, 0] = s0_ref[b, 0] * dl + upd


def _delta_sample(xc, z, ba, conv_w, alog_row, dtb_row, dn_norm, s0, heads, dk, t, nseq):
    rows_all = z.shape[0]
    b = rows_all // t
    kern = functools.partial(_delta_sample_kernel, nseq=nseq, t=t, heads=heads)
    xblk = (nseq * 2 * t, dk)
    cwblk = (CONV_W, dk)
    return pl.pallas_call(
        kern,
        grid=(b // nseq, heads),
        in_specs=[pl.BlockSpec(xblk, lambda i, h: (i, h)),
                  pl.BlockSpec(xblk, lambda i, h: (i, heads + h)),
                  pl.BlockSpec(xblk, lambda i, h: (i, 2 * heads + h)),
                  pl.BlockSpec(cwblk, lambda i, h: (0, h)),
                  pl.BlockSpec(cwblk, lambda i, h: (0, heads + h)),
                  pl.BlockSpec(cwblk, lambda i, h: (0, 2 * heads + h)),
                  pl.BlockSpec((nseq * t, dk), lambda i, h: (i, h)),
                  pl.BlockSpec((nseq * t, LANES), lambda i, h: (i, 0)),
                  _const_spec((1, LANES)), _const_spec((1, LANES)), _const_spec((1, dk)),
                  pl.BlockSpec((nseq, 1, dk, dk), lambda i, h: (i, h, 0, 0))],
        out_specs=(pl.BlockSpec((nseq * t, dk), lambda i, h: (i, h)),
                   pl.BlockSpec((nseq, 1, dk, dk), lambda i, h: (i, h, 0, 0))),
        out_shape=(jax.ShapeDtypeStruct((rows_all, heads * dk), BF16),
                   jax.ShapeDtypeStruct(s0.shape, F32)),
        scratch_shapes=[pltpu.VMEM((nseq * t, dk), F32), pltpu.VMEM((nseq * t, dk), F32)],
        compiler_params=_cparams(2),
        name="delta_sample",
    )(xc, xc, xc, conv_w, conv_w, conv_w, z, ba, alog_row, dtb_row, dn_norm.reshape(1, dk), s0)


def _sgu_kernel(u_ref, v_ref, w_ref, b_ref, o_ref, *, groups, gc, blk):
    rows = u_ref.shape[0]
    ri, ci = _iotas(rows)
    causal = _same_block(ri, ci, blk) & (ri >= ci)
    bias = b_ref[...]
    for g in range(groups):
        wc = jnp.where(causal, w_ref[g], 0.0)
        mixed = _mm(wc, v_ref[:, g * gc:(g + 1) * gc]) + bias[:, g:g + 1]
        o_ref[:, g * gc:(g + 1) * gc] = (u_ref[:, g * gc:(g + 1) * gc] * mixed).astype(BF16)


def _sgu(u, v, w_tiles, b_cols, blk):
    n, width = u.shape
    groups, rows, _ = w_tiles.shape
    kern = functools.partial(_sgu_kernel, groups=groups, gc=width // groups, blk=blk)
    return pl.pallas_call(
        kern,
        grid=(n // rows,),
        in_specs=[pl.BlockSpec((rows, width), lambda i: (i, 0)),
                  pl.BlockSpec((rows, width), lambda i: (i, 0)),
                  _const_spec(w_tiles.shape), _const_spec(b_cols.shape)],
        out_specs=pl.BlockSpec((rows, width), lambda i: (i, 0)),
        out_shape=jax.ShapeDtypeStruct((n, width), BF16),
        compiler_params=_cparams(1),
        name="sgu",
    )(u, v, w_tiles, b_cols)


def _mem_kv_kernel(m_ref, g_ref, w_ref, o_ref):
    o_ref[...] = _mm(_rms(m_ref[...], g_ref[...]), w_ref[...])


def _mem_kv(mem2d, gain, w_kv, tm):
    n, d = mem2d.shape
    return pl.pallas_call(
        _mem_kv_kernel,
        grid=(n // tm,),
        in_specs=[pl.BlockSpec((tm, d), lambda i: (i, 0)), _const_spec((1, d)),
                  _const_spec(w_kv.shape)],
        out_specs=pl.BlockSpec((tm, w_kv.shape[1]), lambda i: (i, 0)),
        out_shape=jax.ShapeDtypeStruct((n, w_kv.shape[1]), F32),
        compiler_params=_cparams(1),
        name="mem_kv",
    )(mem2d, gain.reshape(1, d), w_kv)


def _attend(q, k, v, scale):
    s = _mm_nt(q, k) * scale
    e = jnp.exp(s - jnp.max(s, axis=-1, keepdims=True))
    return _mm(e, v) / jnp.sum(e, axis=-1, keepdims=True)


def _mem_attn_prompt_kernel(q_ref, k_ref, v_ref, o_ref, *, heads, hd):
    scale = hd ** -0.5
    for h in range(heads):
        sl = slice(h * hd, (h + 1) * hd)
        o_ref[0, :, sl] = _attend(q_ref[0, :, sl], k_ref[0, :, sl], v_ref[0, :, sl], scale).astype(BF16)


def _mem_attn_prompt(q, mk, mv, heads, tq):
    b, t, w = q.shape
    m = mk.shape[1]
    kern = functools.partial(_mem_attn_prompt_kernel, heads=heads, hd=w // heads)
    return pl.pallas_call(
        kern,
        grid=(b, t // tq),
        in_specs=[pl.BlockSpec((1, tq, w), lambda i, j: (i, j, 0)),
                  pl.BlockSpec((1, m, w), lambda i, j: (i, 0, 0)),
                  pl.BlockSpec((1, m, w), lambda i, j: (i, 0, 0))],
        out_specs=pl.BlockSpec((1, tq, w), lambda i, j: (i, j, 0)),
        out_shape=jax.ShapeDtypeStruct((b, t, w), BF16),
        compiler_params=_cparams(2),
        name="mem_attn_prompt",
    )(q, mk, mv)


def _mem_attn_sample_kernel(q_ref, k_ref, v_ref, o_ref, *, nseq, t, heads, hd):
    scale = hd ** -0.5
    for b in range(nseq):
        for h in range(heads):
            sl = slice(h * hd, (h + 1) * hd)
            o = _attend(q_ref[b * t:(b + 1) * t, sl], k_ref[b, :, sl], v_ref[b, :, sl], scale)
            o_ref[b * t:(b + 1) * t, sl] = o


def _mem_attn_sample(q2d, mk, mv, heads, t, nseq):
    n, w = q2d.shape
    b, m, _ = mk.shape
    kern = functools.partial(_mem_attn_sample_kernel, nseq=nseq, t=t, heads=heads, hd=w // heads)
    return pl.pallas_call(
        kern,
        grid=(b // nseq,),
        in_specs=[pl.BlockSpec((nseq * t, w), lambda i: (i, 0)),
                  pl.BlockSpec((nseq, m, w), lambda i: (i, 0, 0)),
                  pl.BlockSpec((nseq, m, w), lambda i: (i, 0, 0))],
        out_specs=pl.BlockSpec((nseq * t, w), lambda i: (i, 0)),
        out_shape=jax.ShapeDtypeStruct((n, w), F32),
        compiler_params=_cparams(1),
        name="mem_attn_sample",
    )(q2d, mk, mv)


def _merge_kernel(x_ref, odn_ref, osg_ref, omem_ref, g_ref, wg_ref, wdn_ref, wsg_ref, wmem_ref,
                  wo_ref, o_ref):
    x = x_ref[...]
    d = x.shape[-1]
    xn = _rms(x, g_ref[...]).astype(BF16)
    merged = None
    for idx, (b_ref, w_ref) in enumerate(((odn_ref, wdn_ref), (osg_ref, wsg_ref), (omem_ref, wmem_ref))):
        gate = _sigmoid(jnp.dot(xn, wg_ref[:, idx * d:(idx + 1) * d], preferred_element_type=F32))
        term = gate * _mm(b_ref[...], w_ref[...])
        merged = term if merged is None else merged + term
    o_ref[...] = x + _mm(merged, wo_ref[...])


def _merge(x2d, odn, osg, omem, norm_mix, wg, wdn, wsg, wmem, wo, tm):
    n, d = x2d.shape
    row = lambda a: pl.BlockSpec((tm, a.shape[1]), lambda i: (i, 0))
    return pl.pallas_call(
        _merge_kernel,
        grid=(n // tm,),
        in_specs=[row(x2d), row(odn), row(osg), row(omem), _const_spec((1, d)),
                  _const_spec(wg.shape), _const_spec(wdn.shape), _const_spec(wsg.shape),
                  _const_spec(wmem.shape), _const_spec(wo.shape)],
        out_specs=row(x2d),
        out_shape=jax.ShapeDtypeStruct((n, d), F32),
        compiler_params=_cparams(1),
        name="merge",
    )(x2d, odn, osg, omem, norm_mix.reshape(1, d), wg, wdn, wsg, wmem, wo)


def _ffn_kernel(x_ref, gf_ref, wgu_ref, wd_ref, gl_ref, o_ref):
    x = x_ref[...]
    dff = wd_ref.shape[0]
    hn = _rms(x, gf_ref[...]).astype(BF16)
    hg = jnp.dot(hn, wgu_ref[:, :dff], preferred_element_type=F32)
    hu = jnp.dot(hn, wgu_ref[:, dff:], preferred_element_type=F32)
    x2 = x + _mm(_silu(hg) * hu, wd_ref[...])
    o_ref[...] = _rms(x2, gl_ref[...])


def _ffn(x2d, norm_ffn, wgu, wd, norm_final, tm):
    n, d = x2d.shape
    return pl.pallas_call(
        _ffn_kernel,
        grid=(n // tm,),
        in_specs=[pl.BlockSpec((tm, d), lambda i: (i, 0)), _const_spec((1, d)),
                  _const_spec(wgu.shape), _const_spec(wd.shape), _const_spec((1, d))],
        out_specs=pl.BlockSpec((tm, d), lambda i: (i, 0)),
        out_shape=jax.ShapeDtypeStruct((n, d), F32),
        compiler_params=_cparams(1),
        name="ffn",
    )(x2d, norm_ffn.reshape(1, d), wgu, wd, norm_final.reshape(1, d))


def kernel(x_prompt, x_sample, state_delta, state_conv, cache_mem_k, cache_mem_v, mem_prompt,
           norm_mix, w_in, conv_w, a_log, dt_bias, dn_norm, sgu_norm, sgu_w, sgu_b,
           w_br_dn, w_br_sgu, w_br_mem, w_o, mem_norm, w_mem_kv, norm_ffn, w_gate_up, w_down,
           norm_final):
    depth = w_in.shape[0]
    assert depth == 1, "single-layer stack only"
    bp, tp, d = x_prompt.shape
    bs, ts, _ = x_sample.shape
    heads, dk, dv = state_delta.shape[2:]
    assert dk == dv == LANES and 2 * heads <= LANES and ts == SUBLANES
    kw = heads * dk
    qkv_w = 2 * kw + heads * dv
    sgu_groups, sgu_chunk, _ = sgu_w.shape[1:]
    sgu_width = sgu_norm.shape[-1]
    mem_tokens, mem_heads, mem_hd = cache_mem_k.shape[2:]
    mem_w = mem_heads * mem_hd
    l = 0

    splits = (qkv_w, heads * dv, heads, heads, sgu_width, sgu_width, mem_w, 3 * d)
    o = [0]
    for s in splits:
        o.append(o[-1] + s)
    w = w_in[l]
    ba_pad = jnp.zeros((d, LANES - 2 * heads), w.dtype)
    w1 = jnp.concatenate([w[:, o[0]:o[2]], w[:, o[4]:o[7]], w[:, o[2]:o[4]], ba_pad], axis=1).astype(BF16)
    widths = (qkv_w, heads * dv, sgu_width, sgu_width, mem_w, LANES)
    w_gate = w[:, o[7]:o[8]].astype(BF16)

    lane_pad = lambda vec: jnp.zeros((1, LANES), F32).at[0, heads:2 * heads].set(vec)
    alog_row = lane_pad(a_log[l])
    dtb_row = lane_pad(dt_bias[l])

    wdn = w_br_dn[l].astype(BF16)
    wsg = w_br_sgu[l].astype(BF16)
    wmem = w_br_mem[l].astype(BF16)
    wo = w_o[l].astype(BF16)
    wgu = w_gate_up[l].astype(BF16)
    wd = w_down[l].astype(BF16)
    wkv = w_mem_kv[l].astype(BF16)

    tm = 256

    np_ = bp * tp
    xp2 = x_prompt.reshape(np_, d)
    qkv_p, z_p, ba_p, u_p, v_p, qm_p = _in_proj(xp2, norm_mix[l], w1, sgu_norm[l], widths, tm)
    odn_p, s_p = _delta_prompt(qkv_p.reshape(bp, tp, qkv_w), z_p.reshape(bp, tp, -1),
                               ba_p.reshape(bp, tp, LANES), conv_w[l], alog_row, dtb_row,
                               dn_norm[l], heads, dk, 256)
    conv_p = qkv_p.reshape(bp, tp, qkv_w)[:, tp - (CONV_W - 1):, :]
    osg_p = _sgu(u_p, v_p, sgu_w[l], sgu_b[l].T, SGU_CHUNK)
    kv_p = _mem_kv(mem_prompt.reshape(bp * mem_tokens, d), mem_norm[l], wkv, tm)
    mk_p = kv_p[:, :mem_w].reshape(bp, mem_tokens, mem_w)
    mv_p = kv_p[:, mem_w:].reshape(bp, mem_tokens, mem_w)
    omem_p = _mem_attn_prompt(qm_p.reshape(bp, tp, mem_w), mk_p, mv_p, mem_heads, 512)
    x1_p = _merge(xp2, odn_p.reshape(np_, -1), osg_p, omem_p.reshape(np_, mem_w), norm_mix[l],
                  w_gate, wdn, wsg, wmem, wo, tm)
    y_p = _ffn(x1_p, norm_ffn[l], wgu, wd, norm_final, tm)

    ns = bs * ts
    xs2 = x_sample.reshape(ns, d)
    qkv_s, z_s, ba_s, u_s, v_s, qm_s = _in_proj(xs2, norm_mix[l], w1, sgu_norm[l], widths, tm)
    qkv_s3 = qkv_s.reshape(bs, ts, qkv_w)
    hist = jnp.concatenate(
        [jnp.zeros((bs, ts - (CONV_W - 1), qkv_w), F32), state_conv[l]], axis=1)
    xc_s = jnp.concatenate([hist, qkv_s3], axis=1).reshape(bs * 2 * ts, qkv_w)
    nseq = LANES // ts
    odn_s, s_s = _delta_sample(xc_s, z_s, ba_s, conv_w[l], alog_row, dtb_row, dn_norm[l],
                               state_delta[l], heads, dk, ts, nseq)
    conv_s = qkv_s3[:, ts - (CONV_W - 1):, :]
    reps = SGU_CHUNK // ts
    w_tiles = jnp.tile(sgu_w[l][:, :ts, :ts], (1, reps, reps))
    b_cols = jnp.tile(sgu_b[l][:, :ts], (1, reps)).T
    osg_s = _sgu(u_s, v_s, w_tiles, b_cols, ts)
    omem_s = _mem_attn_sample(qm_s, cache_mem_k[l].reshape(bs, mem_tokens, mem_w),
                              cache_mem_v[l].reshape(bs, mem_tokens, mem_w), mem_heads, ts, 8)
    x1_s = _merge(xs2, odn_s, osg_s, omem_s, norm_mix[l], w_gate, wdn, wsg, wmem, wo, tm)
    y_s = _ffn(x1_s, norm_ffn[l], wgu, wd, norm_final, tm)

    return (y_p.reshape(bp, tp, d), y_s.reshape(bs, ts, d),
            s_p[None], conv_p[None],
            mk_p.reshape(1, bp, mem_tokens, mem_heads, mem_hd),
            mv_p.reshape(1, bp, mem_tokens, mem_heads, mem_hd),
            s_s[None], conv_s[None], v_s.reshape(1, bs, ts, sgu_width))
```

```python
import functools
import math

import jax
import jax.numpy as jnp
from jax import lax
from jax.experimental import pallas as pl
from jax.experimental.pallas import tpu as pltpu

F32 = jnp.float32
BF16 = jnp.bfloat16

RMS_EPS = 1e-6
L2_EPS = 1e-6
CONV_W = 4
DN_CHUNK = 64
SGU_CHUNK = 128
LANES = 128
SUBLANES = 8
VMEM_LIMIT = 56 * 1024 * 1024


def _cparams(n_axes):
    return pltpu.CompilerParams(
        dimension_semantics=("arbitrary",) * n_axes, vmem_limit_bytes=VMEM_LIMIT)


def _mm(a, b):
    return jnp.dot(a.astype(BF16), b.astype(BF16), preferred_element_type=F32)


def _mm_nt(a, b):
    return lax.dot_general(a.astype(BF16), b.astype(BF16), (((1,), (1,)), ((), ())),
                           preferred_element_type=F32)


def _mm_tn(a, b):
    return lax.dot_general(a.astype(BF16), b.astype(BF16), (((0,), (0,)), ((), ())),
                           preferred_element_type=F32)


def _rms(x, gain):
    return x * lax.rsqrt(jnp.mean(x * x, axis=-1, keepdims=True) + RMS_EPS) * gain


def _sigmoid(x):
    return 1.0 / (1.0 + jnp.exp(-x))


def _silu(x):
    return x * _sigmoid(x)


def _softplus(x):
    return jnp.maximum(x, 0.0) + jnp.log1p(jnp.exp(-jnp.abs(x)))


def _const_spec(shape):
    nd = len(shape)
    return pl.BlockSpec(shape, lambda *_: (0,) * nd)


def _l2norm_heads(a, heads, dk, scale):
    cols = []
    for h in range(heads):
        ah = a[:, h * dk:(h + 1) * dk]
        cols.append(ah * (lax.rsqrt(jnp.sum(ah * ah, axis=-1, keepdims=True) + L2_EPS) * scale))
    return jnp.concatenate(cols, axis=1)


def _in_proj_kernel(*refs, offs, heads, dk, sgu_groups, sgu_blk, seq_tiles, conv):
    x_ref, g_ref, w_ref, sgn_ref, sw_ref, sb_ref = refs[:6]
    if conv:
        cw_ref, qkv_ref, zg_ref, ba_ref, osg_ref, qm_ref, tail_ref, xc_ref = refs[6:]
    else:
        qkv_ref, zg_ref, ba_ref, osg_ref, qm_ref, vsg_ref = refs[6:]
    tm = x_ref.shape[0]
    xn = _rms(x_ref[...], g_ref[...]).astype(BF16)

    def proj(k):
        return jnp.dot(xn, w_ref[:, offs[k]:offs[k + 1]], preferred_element_type=F32)

    if conv:
        hd = SUBLANES

        @pl.when(pl.program_id(0) % seq_tiles == 0)
        def _():
            xc_ref[0:hd, :] = jnp.zeros((hd, xc_ref.shape[1]), F32)

        xc_ref[hd:hd + tm, :] = proj(0)
        cw = cw_ref[...]
        y = xc_ref[pl.ds(hd - (CONV_W - 1), tm), :] * cw[0:1, :]
        for i in range(1, CONV_W):
            y = y + xc_ref[pl.ds(hd - (CONV_W - 1) + i, tm), :] * cw[i:i + 1, :]
        last = xc_ref[tm:tm + hd, :]
        tail_ref[0] = last
        xc_ref[0:hd, :] = last
        act = _silu(y)
        kw = heads * dk
        qkv_ref[:, 0:kw] = _l2norm_heads(act[:, 0:kw], heads, dk, dk ** -0.5).astype(BF16)
        qkv_ref[:, kw:2 * kw] = _l2norm_heads(act[:, kw:2 * kw], heads, dk, 1.0).astype(BF16)
        qkv_ref[:, 2 * kw:] = act[:, 2 * kw:].astype(BF16)
    else:
        qkv_ref[...] = proj(0)
    zg_ref[...] = _silu(proj(1)).astype(BF16)
    qm_ref[...] = proj(4).astype(qm_ref.dtype)
    ba_ref[...] = proj(5)

    u = jax.nn.gelu(proj(2))
    v = _rms(jax.nn.gelu(proj(3)), sgn_ref[...])
    if not conv:
        vsg_ref[...] = v
    rows = sw_ref.shape[1]
    ri, ci = _iotas(rows)
    causal = _same_block(ri, ci, sgu_blk) & (ri >= ci)
    bias = sb_ref[...]
    gcw = u.shape[1] // sgu_groups
    vb = v.astype(BF16)
    for g in range(sgu_groups):
        wc = jnp.where(causal, sw_ref[g], 0.0).astype(BF16)
        cs = slice(g * gcw, (g + 1) * gcw)
        for rb in range(tm // rows):
            rs = slice(rb * rows, (rb + 1) * rows)
            mixed = jnp.dot(wc, vb[rs, cs], preferred_element_type=F32) + bias[:, g:g + 1]
            osg_ref[rs, cs] = (u[rs, cs] * mixed).astype(BF16)


def _in_proj(x2d, norm_mix, w1, sgu_norm, sgu_w, sgu_bcols, sgu_blk, widths, tm, heads, dk,
             conv_w=None, seq_len=None):
    n, d = x2d.shape
    conv = conv_w is not None
    offs = (0,) + tuple(sum(widths[:k + 1]) for k in range(len(widths)))
    qkv_w, z_w, sg_w, _, qm_w, ba_w = widths
    row = lambda w_: pl.BlockSpec((tm, w_), lambda i: (i, 0))
    out_shape = [jax.ShapeDtypeStruct((n, qkv_w), BF16 if conv else F32),
                 jax.ShapeDtypeStruct((n, z_w), BF16),
                 jax.ShapeDtypeStruct((n, ba_w), F32),
                 jax.ShapeDtypeStruct((n, sg_w), BF16),
                 jax.ShapeDtypeStruct((n, qm_w), BF16 if conv else F32)]
    out_specs = [row(qkv_w), row(z_w), row(ba_w), row(sg_w), row(qm_w)]
    in_specs = [row(d), _const_spec((1, d)), _const_spec(w1.shape), _const_spec((1, sg_w)),
                _const_spec(sgu_w.shape), _const_spec(sgu_bcols.shape)]
    args = [x2d, norm_mix.reshape(1, d), w1, sgu_norm.reshape(1, -1), sgu_w, sgu_bcols]
    scratch = []
    seq_tiles = 1
    if conv:
        seq_tiles = seq_len // tm
        in_specs.append(_const_spec(conv_w.shape))
        args.append(conv_w)
        out_shape.append(jax.ShapeDtypeStruct((n // seq_len, SUBLANES, qkv_w), F32))
        out_specs.append(pl.BlockSpec((1, SUBLANES, qkv_w), lambda i: (i // seq_tiles, 0, 0)))
        scratch.append(pltpu.VMEM((tm + 2 * SUBLANES, qkv_w), F32))
    else:
        out_shape.append(jax.ShapeDtypeStruct((n, sg_w), F32))
        out_specs.append(row(sg_w))
    kern = functools.partial(_in_proj_kernel, offs=offs, heads=heads, dk=dk,
                             sgu_groups=sgu_w.shape[0], sgu_blk=sgu_blk, seq_tiles=seq_tiles, conv=conv)
    return pl.pallas_call(
        kern,
        grid=(n // tm,),
        in_specs=in_specs,
        out_specs=tuple(out_specs),
        out_shape=tuple(out_shape),
        scratch_shapes=scratch,
        compiler_params=_cparams(1),
        name="in_proj_prompt" if conv else "in_proj_sample",
    )(*args)


def _log2(v):
    r = int(math.log2(v))
    assert 1 << r == v
    return r


def _iotas(n):
    ri = lax.broadcasted_iota(jnp.int32, (n, n), 0)
    ci = lax.broadcasted_iota(jnp.int32, (n, n), 1)
    return ri, ci


def _same_block(ri, ci, size):
    s = _log2(size)
    return (ri >> s) == (ci >> s)


def _unit_lower_inverse(lmats, ri, ci, blk, base):
    eye = (ri == ci).astype(F32)
    if base < blk:
        same = _same_block(ri, ci, base)
        lbs = [jnp.where(same, lm, 0.0) for lm in lmats]
    else:
        lbs = list(lmats)
    ps = lbs
    ts = [eye - lb for lb in lbs]
    k = 2
    while k < base:
        ps = [_mm(p, p) for p in ps]
        ts = [t + _mm(t, p) for t, p in zip(ts, ps)]
        k *= 2
    s = base
    while s < blk:
        sh = _log2(s)
        off = _same_block(ri, ci, 2 * s) & (((ri >> sh) & 1) == 1) & (((ci >> sh) & 1) == 0)
        xs = [_mm(jnp.where(off, lm, 0.0), t) for lm, t in zip(lmats, ts)]
        ts = [t - _mm(t, x) for t, x in zip(ts, xs)]
        s *= 2
    return ts


def _chunk_cumsum(g, chunk):
    n = g.shape[0]
    row = lax.broadcasted_iota(jnp.int32, g.shape, 0) & (chunk - 1)
    s = 1
    while s < chunk:
        shifted = pltpu.roll(g, s, 0)
        g = g + jnp.where(row >= s, shifted, 0.0)
        s *= 2
    return g


def _delta_gates(ba, alog_row, dtb_row, chunk):
    beta = _sigmoid(ba)
    g = -jnp.exp(alog_row) * _softplus(ba + dtb_row)
    gc = _chunk_cumsum(g, chunk)
    return beta, gc, gc.T


def _heads_prep(qs, ks, vs, betas, gcs, gcrows, glasts, causal, strict, ri, ci, blk, base):
    dv = vs[0].shape[-1]
    egs = [jnp.exp(gc) for gc in gcs]
    kbs = [k * b for k, b in zip(ks, betas)]
    vbs = [v * b for v, b in zip(vs, betas)]
    decays = [jnp.exp(jnp.where(causal, gc - gr, -jnp.inf)) for gc, gr in zip(gcs, gcrows)]
    lmats = [jnp.where(strict, _mm_nt(kb, k) * dec, 0.0) for kb, k, dec in zip(kbs, ks, decays)]
    a_intras = [_mm_nt(q, k) * dec for q, k, dec in zip(qs, ks, decays)]
    tmats = _unit_lower_inverse(lmats, ri, ci, blk, base)
    uws = [_mm(t, jnp.concatenate([vb, kb * eg], axis=1))
           for t, vb, kb, eg in zip(tmats, vbs, kbs, egs)]
    us = [uw[:, :dv] for uw in uws]
    ws = [uw[:, dv:] for uw in uws]
    q_decs = [q * eg for q, eg in zip(qs, egs)]
    k_decs = [k * jnp.exp(gl - gc) for k, gl, gc in zip(ks, glasts, gcs)]
    return us, ws, a_intras, q_decs, k_decs


def _out_norm_gate(o, dn_gain, zgate):
    return (_rms(o, dn_gain) * zgate).astype(BF16)


def _delta_prompt_kernel(qkv_ref, z_ref, ba_ref, alog_ref, dtb_ref, dng_ref,
                         o_ref, s_ref, vn_ref, *, tb, heads, dk):
    @pl.when(pl.program_id(1) == 0)
    def _():
        s_ref[...] = jnp.zeros_like(s_ref)

    qkv = qkv_ref[0]
    beta, gc, gct = _delta_gates(ba_ref[0], alog_ref[...], dtb_ref[...], DN_CHUNK)
    nchunk = tb // DN_CHUNK
    glast = jnp.concatenate(
        [jnp.broadcast_to(gc[(c + 1) * DN_CHUNK - 1:(c + 1) * DN_CHUNK, :], (DN_CHUNK, gc.shape[1]))
         for c in range(nchunk)], axis=0)

    ri, ci = _iotas(tb)
    same = _same_block(ri, ci, DN_CHUNK)
    causal = same & (ri >= ci)
    strict = same & (ri > ci)
    kw = heads * dk
    z = z_ref[0]
    dng = dng_ref[...]

    hs = range(heads)
    gcols = [heads + h for h in hs]
    us, ws, a_intras, q_decs, k_decs = _heads_prep(
        [qkv[:, h * dk:(h + 1) * dk].astype(F32) for h in hs],
        [qkv[:, kw + h * dk:kw + (h + 1) * dk].astype(F32) for h in hs],
        [qkv[:, 2 * kw + h * dk:2 * kw + (h + 1) * dk].astype(F32) for h in hs],
        [beta[:, h:h + 1] for h in hs],
        [gc[:, g:g + 1] for g in gcols],
        [gct[g:g + 1, :] for g in gcols],
        [glast[:, g:g + 1] for g in gcols],
        causal, strict, ri, ci, DN_CHUNK, 16)
    vn_ref[...] = jnp.zeros_like(vn_ref)
    outs = [[] for _ in hs]
    for c in range(nchunk):
        r0, r1 = c * DN_CHUNK, (c + 1) * DN_CHUNK
        for h in hs:
            s = s_ref[0, h]
            wq = _mm(jnp.concatenate([ws[h][r0:r1], q_decs[h][r0:r1]], axis=0), s)
            v_new = us[h][r0:r1] - wq[:DN_CHUNK]
            vn_ref[h, r0:r1, :] = v_new
            outs[h].append(wq[DN_CHUNK:] + _mm(a_intras[h][r0:r1, :], vn_ref[h]))
            dl = jnp.exp(glast[r0:r0 + 1, gcols[h]:gcols[h] + 1])
            s_ref[0, h] = s * dl + _mm_tn(k_decs[h][r0:r1], v_new)
    for h in hs:
        o = jnp.concatenate(outs[h], axis=0)
        o_ref[0, :, h * dk:(h + 1) * dk] = _out_norm_gate(o, dng, z[:, h * dk:(h + 1) * dk])


def _delta_prompt(qkv, zg, ba, alog_row, dtb_row, dn_norm, heads, dk, tb):
    b, t, c3 = qkv.shape
    vw = zg.shape[-1]
    kern = functools.partial(_delta_prompt_kernel, tb=tb, heads=heads, dk=dk)
    return pl.pallas_call(
        kern,
        grid=(b, t // tb),
        in_specs=[pl.BlockSpec((1, tb, c3), lambda i, j: (i, j, 0)),
                  pl.BlockSpec((1, tb, vw), lambda i, j: (i, j, 0)),
                  pl.BlockSpec((1, tb, LANES), lambda i, j: (i, j, 0)),
                  _const_spec((1, LANES)), _const_spec((1, LANES)), _const_spec((1, dk))],
        out_specs=(pl.BlockSpec((1, tb, vw), lambda i, j: (i, j, 0)),
                   pl.BlockSpec((1, heads, dk, dk), lambda i, j: (i, 0, 0, 0))),
        out_shape=(jax.ShapeDtypeStruct((b, t, vw), BF16),
                   jax.ShapeDtypeStruct((b, heads, dk, dk), F32)),
        scratch_shapes=[pltpu.VMEM((heads, tb, dk), F32)],
        compiler_params=_cparams(2),
        name="delta_prompt",
    )(qkv, zg, ba, alog_row, dtb_row, dn_norm.reshape(1, dk))


def _delta_sample_kernel(xq_ref, xk_ref, xv_ref, cwq_ref, cwk_ref, cwv_ref, z_ref, ba_ref,
                         alog_ref, dtb_ref, dng_ref, s0_ref, o_ref, s_ref, vn_ref, qs_ref,
                         *, nseq, t, heads):
    h = pl.program_id(1)
    rows = nseq * t
    grp = 2 * t

    def conv(x_ref, cw_ref):
        x = x_ref[...]
        cw = cw_ref[...]
        y = x * cw[CONV_W - 1:CONV_W, :]
        for i in range(CONV_W - 1):
            y = y + pltpu.roll(x, CONV_W - 1 - i, 0) * cw[i:i + 1, :]
        y = y.reshape(nseq, grp, y.shape[-1])[:, t:, :].reshape(rows, y.shape[-1])
        return _silu(y)

    q = _l2norm_heads(conv(xq_ref, cwq_ref), 1, xq_ref.shape[1], xq_ref.shape[1] ** -0.5)
    k = _l2norm_heads(conv(xk_ref, cwk_ref), 1, xk_ref.shape[1], 1.0)
    v = conv(xv_ref, cwv_ref)

    beta, gc, gct = _delta_gates(ba_ref[...], alog_ref[...], dtb_ref[...], t)
    lane = lax.broadcasted_iota(jnp.int32, (rows, LANES), 1)
    beta_h = jnp.sum(jnp.where(lane == h, beta, 0.0), axis=1, keepdims=True)
    gc_h = jnp.sum(jnp.where(lane == heads + h, gc, 0.0), axis=1, keepdims=True)
    sub = lax.broadcasted_iota(jnp.int32, (LANES, rows), 0)
    gcrow_h = jnp.sum(jnp.where(sub == heads + h, gct, 0.0), axis=0, keepdims=True)
    glast_h = jnp.concatenate(
        [jnp.broadcast_to(gc_h[(b + 1) * t - 1:(b + 1) * t, :], (t, 1)) for b in range(nseq)], axis=0)

    ri, ci = _iotas(rows)
    same = _same_block(ri, ci, t)
    causal = same & (ri >= ci)
    strict = same & (ri > ci)
    (u,), (w,), (a_intra,), (q_dec,), (k_dec,) = _heads_prep(
        [q], [k], [v], [beta_h], [gc_h], [gcrow_h], [glast_h], causal, strict, ri, ci, t, t)

    for b in range(nseq):
        r0, r1 = b * t, (b + 1) * t
        wq = _mm(jnp.concatenate([w[r0:r1], q_dec[r0:r1]], axis=0), s0_ref[b, 0])
        vn_ref[r0:r1, :] = u[r0:r1] - wq[:t]
        qs_ref[r0:r1, :] = wq[t:]
    v_new = vn_ref[...]
    o = qs_ref[...] + _mm(a_intra, v_new)
    o_ref[...] = _out_norm_gate(o, dng_ref[...], z_ref[...])

    kt = k_dec.T
    col_seq = lax.broadcasted_iota(jnp.int32, kt.shape, 1) >> _log2(t)
    v_new_b = v_new.astype(BF16)
    for b in range(nseq):
        dl = jnp.exp(glast_h[b * t:b * t + 1, :])
        upd = jnp.dot(jnp.where(col_seq == b, kt, 0.0).astype(BF16), v_new_b,
                      preferred_element_type=F32)
        s_ref[b, 0] = s0_ref[b, 0] * dl + upd


def _delta_sample(xc, z, ba, conv_w, alog_row, dtb_row, dn_norm, s0, heads, dk, t, nseq):
    rows_all = z.shape[0]
    b = rows_all // t
    kern = functools.partial(_delta_sample_kernel, nseq=nseq, t=t, heads=heads)
    xblk = (nseq * 2 * t, dk)
    cwblk = (CONV_W, dk)
    return pl.pallas_call(
        kern,
        grid=(b // nseq, heads),
        in_specs=[pl.BlockSpec(xblk, lambda i, h: (i, h)),
                  pl.BlockSpec(xblk, lambda i, h: (i, heads + h)),
                  pl.BlockSpec(xblk, lambda i, h: (i, 2 * heads + h)),
                  pl.BlockSpec(cwblk, lambda i, h: (0, h)),
                  pl.BlockSpec(cwblk, lambda i, h: (0, heads + h)),
                  pl.BlockSpec(cwblk, lambda i, h: (0, 2 * heads + h)),
                  pl.BlockSpec((nseq * t, dk), lambda i, h: (i, h)),
                  pl.BlockSpec((nseq * t, LANES), lambda i, h: (i, 0)),
                  _const_spec((1, LANES)), _const_spec((1, LANES)), _const_spec((1, dk)),
                  pl.BlockSpec((nseq, 1, dk, dk), lambda i, h: (i, h, 0, 0))],
        out_specs=(pl.BlockSpec((nseq * t, dk), lambda i, h: (i, h)),
                   pl.BlockSpec((nseq, 1, dk, dk), lambda i, h: (i, h, 0, 0))),
        out_shape=(jax.ShapeDtypeStruct((rows_all, heads * dk), BF16),
                   jax.ShapeDtypeStruct(s0.shape, F32)),
        scratch_shapes=[pltpu.VMEM((nseq * t, dk), F32), pltpu.VMEM((nseq * t, dk), F32)],
        compiler_params=_cparams(2),
        name="delta_sample",
    )(xc, xc, xc, conv_w, conv_w, conv_w, z, ba, alog_row, dtb_row, dn_norm.reshape(1, dk), s0)


def _mem_kv_kernel(m_ref, g_ref, w_ref, o_ref):
    o_ref[...] = _mm(_rms(m_ref[...], g_ref[...]), w_ref[...])


def _mem_kv(mem2d, gain, w_kv, tm):
    n, d = mem2d.shape
    return pl.pallas_call(
        _mem_kv_kernel,
        grid=(n // tm,),
        in_specs=[pl.BlockSpec((tm, d), lambda i: (i, 0)), _const_spec((1, d)),
                  _const_spec(w_kv.shape)],
        out_specs=pl.BlockSpec((tm, w_kv.shape[1]), lambda i: (i, 0)),
        out_shape=jax.ShapeDtypeStruct((n, w_kv.shape[1]), F32),
        compiler_params=_cparams(1),
        name="mem_kv",
    )(mem2d, gain.reshape(1, d), w_kv)


def _attend(q, k, v, scale):
    s = _mm_nt(q, k) * scale
    e = jnp.exp(s - jnp.max(s, axis=-1, keepdims=True))
    return _mm(e, v) / jnp.sum(e, axis=-1, keepdims=True)


def _mem_attn_prompt_kernel(q_ref, k_ref, v_ref, o_ref, *, heads, hd):
    scale = hd ** -0.5
    for h in range(heads):
        sl = slice(h * hd, (h + 1) * hd)
        o_ref[0, :, sl] = _attend(q_ref[0, :, sl], k_ref[0, :, sl], v_ref[0, :, sl], scale).astype(BF16)


def _mem_attn_prompt(q, mk, mv, heads, tq):
    b, t, w = q.shape
    m = mk.shape[1]
    kern = functools.partial(_mem_attn_prompt_kernel, heads=heads, hd=w // heads)
    return pl.pallas_call(
        kern,
        grid=(b, t // tq),
        in_specs=[pl.BlockSpec((1, tq, w), lambda i, j: (i, j, 0)),
                  pl.BlockSpec((1, m, w), lambda i, j: (i, 0, 0)),
                  pl.BlockSpec((1, m, w), lambda i, j: (i, 0, 0))],
        out_specs=pl.BlockSpec((1, tq, w), lambda i, j: (i, j, 0)),
        out_shape=jax.ShapeDtypeStruct((b, t, w), BF16),
        compiler_params=_cparams(2),
        name="mem_attn_prompt",
    )(q, mk, mv)


def _mem_attn_sample_kernel(q_ref, k_ref, v_ref, o_ref, *, nseq, t, heads, hd):
    scale = hd ** -0.5
    for b in range(nseq):
        for h in range(heads):
            sl = slice(h * hd, (h + 1) * hd)
            o = _attend(q_ref[b * t:(b + 1) * t, sl], k_ref[b, :, sl], v_ref[b, :, sl], scale)
            o_ref[b * t:(b + 1) * t, sl] = o


def _mem_attn_sample(q2d, mk, mv, heads, t, nseq):
    n, w = q2d.shape
    b, m, _ = mk.shape
    kern = functools.partial(_mem_attn_sample_kernel, nseq=nseq, t=t, heads=heads, hd=w // heads)
    return pl.pallas_call(
        kern,
        grid=(b // nseq,),
        in_specs=[pl.BlockSpec((nseq * t, w), lambda i: (i, 0)),
                  pl.BlockSpec((nseq, m, w), lambda i: (i, 0, 0)),
                  pl.BlockSpec((nseq, m, w), lambda i: (i, 0, 0))],
        out_specs=pl.BlockSpec((nseq * t, w), lambda i: (i, 0)),
        out_shape=jax.ShapeDtypeStruct((n, w), F32),
        compiler_params=_cparams(1),
        name="mem_attn_sample",
    )(q2d, mk, mv)


def _merge_kernel(x_ref, odn_ref, osg_ref, omem_ref, g_ref, wg_ref, wdn_ref, wsg_ref, wmem_ref,
                  wo_ref, o_ref):
    x = x_ref[...]
    d = x.shape[-1]
    xn = _rms(x, g_ref[...]).astype(BF16)
    merged = None
    for idx, (b_ref, w_ref) in enumerate(((odn_ref, wdn_ref), (osg_ref, wsg_ref), (omem_ref, wmem_ref))):
        gate = _sigmoid(jnp.dot(xn, wg_ref[:, idx * d:(idx + 1) * d], preferred_element_type=F32))
        term = gate * _mm(b_ref[...], w_ref[...])
        merged = term if merged is None else merged + term
    o_ref[...] = x + _mm(merged, wo_ref[...])


def _merge(x2d, odn, osg, omem, norm_mix, wg, wdn, wsg, wmem, wo, tm):
    n, d = x2d.shape
    row = lambda a: pl.BlockSpec((tm, a.shape[1]), lambda i: (i, 0))
    return pl.pallas_call(
        _merge_kernel,
        grid=(n // tm,),
        in_specs=[row(x2d), row(odn), row(osg), row(omem), _const_spec((1, d)),
                  _const_spec(wg.shape), _const_spec(wdn.shape), _const_spec(wsg.shape),
                  _const_spec(wmem.shape), _const_spec(wo.shape)],
        out_specs=row(x2d),
        out_shape=jax.ShapeDtypeStruct((n, d), F32),
        compiler_params=_cparams(1),
        name="merge",
    )(x2d, odn, osg, omem, norm_mix.reshape(1, d), wg, wdn, wsg, wmem, wo)


def _ffn_kernel(x_ref, gf_ref, wgu_ref, wd_ref, gl_ref, o_ref):
    x = x_ref[...]
    dff = wd_ref.shape[0]
    hn = _rms(x, gf_ref[...]).astype(BF16)
    hg = jnp.dot(hn, wgu_ref[:, :dff], preferred_element_type=F32)
    hu = jnp.dot(hn, wgu_ref[:, dff:], preferred_element_type=F32)
    x2 = x + _mm(_silu(hg) * hu, wd_ref[...])
    o_ref[...] = _rms(x2, gl_ref[...])


def _ffn(x2d, norm_ffn, wgu, wd, norm_final, tm):
    n, d = x2d.shape
    return pl.pallas_call(
        _ffn_kernel,
        grid=(n // tm,),
        in_specs=[pl.BlockSpec((tm, d), lambda i: (i, 0)), _const_spec((1, d)),
                  _const_spec(wgu.shape), _const_spec(wd.shape), _const_spec((1, d))],
        out_specs=pl.BlockSpec((tm, d), lambda i: (i, 0)),
        out_shape=jax.ShapeDtypeStruct((n, d), F32),
        compiler_params=_cparams(1),
        name="ffn",
    )(x2d, norm_ffn.reshape(1, d), wgu, wd, norm_final.reshape(1, d))


def kernel(x_prompt, x_sample, state_delta, state_conv, cache_mem_k, cache_mem_v, mem_prompt,
           norm_mix, w_in, conv_w, a_log, dt_bias, dn_norm, sgu_norm, sgu_w, sgu_b,
           w_br_dn, w_br_sgu, w_br_mem, w_o, mem_norm, w_mem_kv, norm_ffn, w_gate_up, w_down,
           norm_final):
    depth = w_in.shape[0]
    assert depth == 1, "single-layer stack only"
    bp, tp, d = x_prompt.shape
    bs, ts, _ = x_sample.shape
    heads, dk, dv = state_delta.shape[2:]
    assert dk == dv == LANES and 2 * heads <= LANES and ts == SUBLANES
    kw = heads * dk
    qkv_w = 2 * kw + heads * dv
    sgu_groups, sgu_chunk, _ = sgu_w.shape[1:]
    sgu_width = sgu_norm.shape[-1]
    mem_tokens, mem_heads, mem_hd = cache_mem_k.shape[2:]
    mem_w = mem_heads * mem_hd
    l = 0

    splits = (qkv_w, heads * dv, heads, heads, sgu_width, sgu_width, mem_w, 3 * d)
    o = [0]
    for s in splits:
        o.append(o[-1] + s)
    w = w_in[l]
    ba_pad = jnp.zeros((d, LANES - 2 * heads), w.dtype)
    w1 = jnp.concatenate([w[:, o[0]:o[2]], w[:, o[4]:o[7]], w[:, o[2]:o[4]], ba_pad], axis=1).astype(BF16)
    widths = (qkv_w, heads * dv, sgu_width, sgu_width, mem_w, LANES)
    w_gate = w[:, o[7]:o[8]].astype(BF16)

    lane_pad = lambda vec: jnp.zeros((1, LANES), F32).at[0, heads:2 * heads].set(vec)
    alog_row = lane_pad(a_log[l])
    dtb_row = lane_pad(dt_bias[l])

    wdn = w_br_dn[l].astype(BF16)
    wsg = w_br_sgu[l].astype(BF16)
    wmem = w_br_mem[l].astype(BF16)
    wo = w_o[l].astype(BF16)
    wgu = w_gate_up[l].astype(BF16)
    wd = w_down[l].astype(BF16)
    wkv = w_mem_kv[l].astype(BF16)

    tm = 256

    np_ = bp * tp
    xp2 = x_prompt.reshape(np_, d)
    qkv_p, zg_p, ba_p, osg_p, qm_p, tail_p = _in_proj(
        xp2, norm_mix[l], w1, sgu_norm[l], sgu_w[l], sgu_b[l].T, SGU_CHUNK, widths, tm, heads, dk,
        conv_w=conv_w[l], seq_len=tp)
    odn_p, s_p = _delta_prompt(qkv_p.reshape(bp, tp, qkv_w), zg_p.reshape(bp, tp, -1),
                               ba_p.reshape(bp, tp, LANES), alog_row, dtb_row,
                               dn_norm[l], heads, dk, 256)
    conv_p = tail_p[:, SUBLANES - (CONV_W - 1):, :]
    kv_p = _mem_kv(mem_prompt.reshape(bp * mem_tokens, d), mem_norm[l], wkv, tm)
    mk_p = kv_p[:, :mem_w].reshape(bp, mem_tokens, mem_w)
    mv_p = kv_p[:, mem_w:].reshape(bp, mem_tokens, mem_w)
    omem_p = _mem_attn_prompt(qm_p.reshape(bp, tp, mem_w), mk_p, mv_p, mem_heads, 512)
    x1_p = _merge(xp2, odn_p.reshape(np_, -1), osg_p, omem_p.reshape(np_, mem_w), norm_mix[l],
                  w_gate, wdn, wsg, wmem, wo, tm)
    y_p = _ffn(x1_p, norm_ffn[l], wgu, wd, norm_final, tm)

    ns = bs * ts
    xs2 = x_sample.reshape(ns, d)
    reps = SGU_CHUNK // ts
    w_tiles = jnp.tile(sgu_w[l][:, :ts, :ts], (1, reps, reps))
    b_cols = jnp.tile(sgu_b[l][:, :ts], (1, reps)).T
    qkv_s, zg_s, ba_s, osg_s, qm_s, v_s = _in_proj(
        xs2, norm_mix[l], w1, sgu_norm[l], w_tiles, b_cols, ts, widths, tm, heads, dk)
    qkv_s3 = qkv_s.reshape(bs, ts, qkv_w)
    hist = jnp.concatenate(
        [jnp.zeros((bs, ts - (CONV_W - 1), qkv_w), F32), state_conv[l]], axis=1)
    xc_s = jnp.concatenate([hist, qkv_s3], axis=1).reshape(bs * 2 * ts, qkv_w)
    nseq = LANES // ts
    odn_s, s_s = _delta_sample(xc_s, zg_s, ba_s, conv_w[l], alog_row, dtb_row, dn_norm[l],
                               state_delta[l], heads, dk, ts, nseq)
    conv_s = qkv_s3[:, ts - (CONV_W - 1):, :]
    omem_s = _mem_attn_sample(qm_s, cache_mem_k[l].reshape(bs, mem_tokens, mem_w),
                              cache_mem_v[l].reshape(bs, mem_tokens, mem_w), mem_heads, ts, 8)
    x1_s = _merge(xs2, odn_s, osg_s, omem_s, norm_mix[l], w_gate, wdn, wsg, wmem, wo, tm)
    y_s = _ffn(x1_s, norm_ffn[l], wgu, wd, norm_final, tm)

    return (y_p.reshape(bp, tp, d), y_s.reshape(bs, ts, d),
            s_p[None], conv_p[None],
            mk_p.reshape(1, bp, mem_tokens, mem_heads, mem_hd),
            mv_p.reshape(1, bp, mem_tokens, mem_heads, mem_hd),
            s_s[None], conv_s[None], v_s.reshape(1, bs, ts, sgu_width))
```

```python
import functools
import math

import jax
import jax.numpy as jnp
from jax import lax
from jax.experimental import pallas as pl
from jax.experimental.pallas import tpu as pltpu

F32 = jnp.float32
BF16 = jnp.bfloat16

RMS_EPS = 1e-6
L2_EPS = 1e-6
CONV_W = 4
DN_CHUNK = 64
SGU_CHUNK = 128
LANES = 128
SUBLANES = 8
VMEM_LIMIT = 56 * 1024 * 1024


def _cparams(n_axes):
    return pltpu.CompilerParams(
        dimension_semantics=("arbitrary",) * n_axes, vmem_limit_bytes=VMEM_LIMIT)


def _mm(a, b):
    return jnp.dot(a.astype(BF16), b.astype(BF16), preferred_element_type=F32)


def _mm_nt(a, b):
    return lax.dot_general(a.astype(BF16), b.astype(BF16), (((1,), (1,)), ((), ())),
                           preferred_element_type=F32)


def _mm_tn(a, b):
    return lax.dot_general(a.astype(BF16), b.astype(BF16), (((0,), (0,)), ((), ())),
                           preferred_element_type=F32)


def _rms(x, gain):
    return x * lax.rsqrt(jnp.mean(x * x, axis=-1, keepdims=True) + RMS_EPS) * gain


def _sigmoid(x):
    return 1.0 / (1.0 + jnp.exp(-x))


def _silu(x):
    return x * _sigmoid(x)


def _softplus(x):
    return jnp.maximum(x, 0.0) + jnp.log1p(jnp.exp(-jnp.abs(x)))


def _const_spec(shape):
    nd = len(shape)
    return pl.BlockSpec(shape, lambda *_: (0,) * nd)


def _l2norm_heads(a, heads, dk, scale):
    cols = []
    for h in range(heads):
        ah = a[:, h * dk:(h + 1) * dk]
        cols.append(ah * (lax.rsqrt(jnp.sum(ah * ah, axis=-1, keepdims=True) + L2_EPS) * scale))
    return jnp.concatenate(cols, axis=1)


def _in_proj_kernel(*refs, offs, heads, dk, sgu_groups, sgu_blk, seq_tiles, conv):
    x_ref, g_ref, wa_ref, wb_ref, sgn_ref, sw_ref, sb_ref = refs[:7]
    if conv:
        cw_ref, qkv_ref, zg_ref, ba_ref, osg_ref, qm_ref, tail_ref, xc_ref = refs[7:]
    else:
        qkv_ref, zg_ref, ba_ref, osg_ref, qm_ref, vsg_ref = refs[7:]
    tm = x_ref.shape[0]
    xn = _rms(x_ref[...], g_ref[...]).astype(BF16)

    def proj(k):
        w_ref, lo, hi = (wa_ref, wb_ref)[offs[k][0]], offs[k][1], offs[k][2]
        return jnp.dot(xn, w_ref[:, lo:hi], preferred_element_type=F32)

    if conv:
        hd = SUBLANES

        @pl.when(pl.program_id(0) % seq_tiles == 0)
        def _():
            xc_ref[0:hd, :] = jnp.zeros((hd, xc_ref.shape[1]), F32)

        xc_ref[hd:hd + tm, :] = proj(0)
        cw = cw_ref[...]
        y = xc_ref[pl.ds(hd - (CONV_W - 1), tm), :] * cw[0:1, :]
        for i in range(1, CONV_W):
            y = y + xc_ref[pl.ds(hd - (CONV_W - 1) + i, tm), :] * cw[i:i + 1, :]
        last = xc_ref[tm:tm + hd, :]
        tail_ref[0] = last
        xc_ref[0:hd, :] = last
        act = _silu(y)
        kw = heads * dk
        qkv_ref[:, 0:kw] = _l2norm_heads(act[:, 0:kw], heads, dk, dk ** -0.5).astype(BF16)
        qkv_ref[:, kw:2 * kw] = _l2norm_heads(act[:, kw:2 * kw], heads, dk, 1.0).astype(BF16)
        qkv_ref[:, 2 * kw:] = act[:, 2 * kw:].astype(BF16)
    else:
        qkv_ref[...] = proj(0)
    zg_ref[...] = _silu(proj(1)).astype(BF16)
    qm_ref[...] = proj(4).astype(qm_ref.dtype)
    ba_ref[...] = proj(5)

    u = jax.nn.gelu(proj(2))
    v = _rms(jax.nn.gelu(proj(3)), sgn_ref[...])
    if not conv:
        vsg_ref[...] = v
    rows = sw_ref.shape[1]
    ri, ci = _iotas(rows)
    causal = _same_block(ri, ci, sgu_blk) & (ri >= ci)
    bias = sb_ref[...]
    gcw = u.shape[1] // sgu_groups
    vb = v.astype(BF16)
    for g in range(sgu_groups):
        wc = jnp.where(causal, sw_ref[g], 0.0).astype(BF16)
        cs = slice(g * gcw, (g + 1) * gcw)
        for rb in range(tm // rows):
            rs = slice(rb * rows, (rb + 1) * rows)
            mixed = jnp.dot(wc, vb[rs, cs], preferred_element_type=F32) + bias[:, g:g + 1]
            osg_ref[rs, cs] = (u[rs, cs] * mixed).astype(BF16)


def _in_proj(x2d, norm_mix, w_all, w_b, sgu_norm, sgu_w, sgu_bcols, sgu_blk, widths, tm, heads, dk,
             conv_w=None, seq_len=None):
    n, d = x2d.shape
    conv = conv_w is not None
    qkv_w, z_w, sg_w, _, qm_w, ba_w = widths
    offs = ((0, 0, qkv_w), (0, qkv_w, qkv_w + z_w))
    lo = 0
    for w_ in widths[2:]:
        offs += ((1, lo, lo + w_),)
        lo += w_
    assert lo == w_b.shape[1]
    row = lambda w_: pl.BlockSpec((tm, w_), lambda i: (i, 0))
    out_shape = [jax.ShapeDtypeStruct((n, qkv_w), BF16 if conv else F32),
                 jax.ShapeDtypeStruct((n, z_w), BF16),
                 jax.ShapeDtypeStruct((n, ba_w), F32),
                 jax.ShapeDtypeStruct((n, sg_w), BF16),
                 jax.ShapeDtypeStruct((n, qm_w), BF16 if conv else F32)]
    out_specs = [row(qkv_w), row(z_w), row(ba_w), row(sg_w), row(qm_w)]
    in_specs = [row(d), _const_spec((1, d)), _const_spec((d, qkv_w + z_w)), _const_spec(w_b.shape),
                _const_spec((1, sg_w)), _const_spec(sgu_w.shape), _const_spec(sgu_bcols.shape)]
    args = [x2d, norm_mix.reshape(1, d), w_all, w_b, sgu_norm.reshape(1, -1), sgu_w, sgu_bcols]
    scratch = []
    seq_tiles = 1
    if conv:
        seq_tiles = seq_len // tm
        in_specs.append(_const_spec(conv_w.shape))
        args.append(conv_w)
        out_shape.append(jax.ShapeDtypeStruct((n // seq_len, SUBLANES, qkv_w), F32))
        out_specs.append(pl.BlockSpec((1, SUBLANES, qkv_w), lambda i: (i // seq_tiles, 0, 0)))
        scratch.append(pltpu.VMEM((tm + 2 * SUBLANES, qkv_w), F32))
    else:
        out_shape.append(jax.ShapeDtypeStruct((n, sg_w), F32))
        out_specs.append(row(sg_w))
    kern = functools.partial(_in_proj_kernel, offs=offs, heads=heads, dk=dk,
                             sgu_groups=sgu_w.shape[0], sgu_blk=sgu_blk, seq_tiles=seq_tiles, conv=conv)
    return pl.pallas_call(
        kern,
        grid=(n // tm,),
        in_specs=in_specs,
        out_specs=tuple(out_specs),
        out_shape=tuple(out_shape),
        scratch_shapes=scratch,
        compiler_params=_cparams(1),
        name="in_proj_prompt" if conv else "in_proj_sample",
    )(*args)


def _log2(v):
    r = int(math.log2(v))
    assert 1 << r == v
    return r


def _iotas(n):
    ri = lax.broadcasted_iota(jnp.int32, (n, n), 0)
    ci = lax.broadcasted_iota(jnp.int32, (n, n), 1)
    return ri, ci


def _same_block(ri, ci, size):
    s = _log2(size)
    return (ri >> s) == (ci >> s)


def _unit_lower_inverse(lmats, ri, ci, blk, base):
    eye = (ri == ci).astype(F32)
    if base < blk:
        same = _same_block(ri, ci, base)
        lbs = [jnp.where(same, lm, 0.0) for lm in lmats]
    else:
        lbs = list(lmats)
    ps = lbs
    ts = [eye - lb for lb in lbs]
    k = 2
    while k < base:
        ps = [_mm(p, p) for p in ps]
        ts = [t + _mm(t, p) for t, p in zip(ts, ps)]
        k *= 2
    s = base
    while s < blk:
        sh = _log2(s)
        off = _same_block(ri, ci, 2 * s) & (((ri >> sh) & 1) == 1) & (((ci >> sh) & 1) == 0)
        xs = [_mm(jnp.where(off, lm, 0.0), t) for lm, t in zip(lmats, ts)]
        ts = [t - _mm(t, x) for t, x in zip(ts, xs)]
        s *= 2
    return ts


def _chunk_cumsum(g, chunk):
    n = g.shape[0]
    row = lax.broadcasted_iota(jnp.int32, g.shape, 0) & (chunk - 1)
    s = 1
    while s < chunk:
        shifted = pltpu.roll(g, s, 0)
        g = g + jnp.where(row >= s, shifted, 0.0)
        s *= 2
    return g


def _delta_gates(ba, alog_row, dtb_row, chunk):
    beta = _sigmoid(ba)
    g = -jnp.exp(alog_row) * _softplus(ba + dtb_row)
    gc = _chunk_cumsum(g, chunk)
    return beta, gc, gc.T


def _heads_prep(qs, ks, vs, betas, gcs, gcrows, glasts, causal, strict, ri, ci, blk, base):
    dv = vs[0].shape[-1]
    egs = [jnp.exp(gc) for gc in gcs]
    kbs = [k * b for k, b in zip(ks, betas)]
    vbs = [v * b for v, b in zip(vs, betas)]
    decays = [jnp.exp(jnp.where(causal, gc - gr, -jnp.inf)) for gc, gr in zip(gcs, gcrows)]
    lmats = [jnp.where(strict, _mm_nt(kb, k) * dec, 0.0) for kb, k, dec in zip(kbs, ks, decays)]
    a_intras = [_mm_nt(q, k) * dec for q, k, dec in zip(qs, ks, decays)]
    tmats = _unit_lower_inverse(lmats, ri, ci, blk, base)
    uws = [_mm(t, jnp.concatenate([vb, kb * eg], axis=1))
           for t, vb, kb, eg in zip(tmats, vbs, kbs, egs)]
    us = [uw[:, :dv] for uw in uws]
    ws = [uw[:, dv:] for uw in uws]
    q_decs = [q * eg for q, eg in zip(qs, egs)]
    k_decs = [k * jnp.exp(gl - gc) for k, gl, gc in zip(ks, glasts, gcs)]
    return us, ws, a_intras, q_decs, k_decs


def _out_norm_gate(o, dn_gain, zgate):
    return (_rms(o, dn_gain) * zgate).astype(BF16)


def _delta_prompt_kernel(qkv_ref, z_ref, ba_ref, alog_ref, dtb_ref, dng_ref,
                         o_ref, s_ref, vn_ref, *, tb, heads, dk):
    @pl.when(pl.program_id(1) == 0)
    def _():
        s_ref[...] = jnp.zeros_like(s_ref)

    qkv = qkv_ref[0]
    beta, gc, gct = _delta_gates(ba_ref[0], alog_ref[...], dtb_ref[...], DN_CHUNK)
    nchunk = tb // DN_CHUNK
    glast = jnp.concatenate(
        [jnp.broadcast_to(gc[(c + 1) * DN_CHUNK - 1:(c + 1) * DN_CHUNK, :], (DN_CHUNK, gc.shape[1]))
         for c in range(nchunk)], axis=0)

    ri, ci = _iotas(tb)
    same = _same_block(ri, ci, DN_CHUNK)
    causal = same & (ri >= ci)
    strict = same & (ri > ci)
    kw = heads * dk
    z = z_ref[0]
    dng = dng_ref[...]

    hs = range(heads)
    gcols = [heads + h for h in hs]
    us, ws, a_intras, q_decs, k_decs = _heads_prep(
        [qkv[:, h * dk:(h + 1) * dk].astype(F32) for h in hs],
        [qkv[:, kw + h * dk:kw + (h + 1) * dk].astype(F32) for h in hs],
        [qkv[:, 2 * kw + h * dk:2 * kw + (h + 1) * dk].astype(F32) for h in hs],
        [beta[:, h:h + 1] for h in hs],
        [gc[:, g:g + 1] for g in gcols],
        [gct[g:g + 1, :] for g in gcols],
        [glast[:, g:g + 1] for g in gcols],
        causal, strict, ri, ci, DN_CHUNK, 16)
    vn_ref[...] = jnp.zeros_like(vn_ref)
    outs = [[] for _ in hs]
    for c in range(nchunk):
        r0, r1 = c * DN_CHUNK, (c + 1) * DN_CHUNK
        for h in hs:
            s = s_ref[0, h]
            wq = _mm(jnp.concatenate([ws[h][r0:r1], q_decs[h][r0:r1]], axis=0), s)
            v_new = us[h][r0:r1] - wq[:DN_CHUNK]
            vn_ref[h, r0:r1, :] = v_new
            outs[h].append(wq[DN_CHUNK:] + _mm(a_intras[h][r0:r1, :], vn_ref[h]))
            dl = jnp.exp(glast[r0:r0 + 1, gcols[h]:gcols[h] + 1])
            s_ref[0, h] = s * dl + _mm_tn(k_decs[h][r0:r1], v_new)
    for h in hs:
        o = jnp.concatenate(outs[h], axis=0)
        o_ref[0, :, h * dk:(h + 1) * dk] = _out_norm_gate(o, dng, z[:, h * dk:(h + 1) * dk])


def _delta_prompt(qkv, zg, ba, alog_row, dtb_row, dn_norm, heads, dk, tb):
    b, t, c3 = qkv.shape
    vw = zg.shape[-1]
    kern = functools.partial(_delta_prompt_kernel, tb=tb, heads=heads, dk=dk)
    return pl.pallas_call(
        kern,
        grid=(b, t // tb),
        in_specs=[pl.BlockSpec((1, tb, c3), lambda i, j: (i, j, 0)),
                  pl.BlockSpec((1, tb, vw), lambda i, j: (i, j, 0)),
                  pl.BlockSpec((1, tb, LANES), lambda i, j: (i, j, 0)),
                  _const_spec((1, LANES)), _const_spec((1, LANES)), _const_spec((1, dk))],
        out_specs=(pl.BlockSpec((1, tb, vw), lambda i, j: (i, j, 0)),
                   pl.BlockSpec((1, heads, dk, dk), lambda i, j: (i, 0, 0, 0))),
        out_shape=(jax.ShapeDtypeStruct((b, t, vw), BF16),
                   jax.ShapeDtypeStruct((b, heads, dk, dk), F32)),
        scratch_shapes=[pltpu.VMEM((heads, tb, dk), F32)],
        compiler_params=_cparams(2),
        name="delta_prompt",
    )(qkv, zg, ba, alog_row, dtb_row, dn_norm.reshape(1, dk))


def _delta_sample_kernel(xq_ref, xk_ref, xv_ref, cwq_ref, cwk_ref, cwv_ref, z_ref, ba_ref,
                         alog_ref, dtb_ref, dng_ref, s0_ref, o_ref, s_ref, vn_ref, qs_ref,
                         *, nseq, t, heads, hg, dk):
    hb = pl.program_id(1) * hg
    rows = nseq * t
    grp = 2 * t
    js = range(hg)

    def conv(x_ref, cw_ref):
        x = x_ref[...]
        cw = cw_ref[...]
        y = x * cw[CONV_W - 1:CONV_W, :]
        for i in range(CONV_W - 1):
            y = y + pltpu.roll(x, CONV_W - 1 - i, 0) * cw[i:i + 1, :]
        y = y.reshape(nseq, grp, y.shape[-1])[:, t:, :].reshape(rows, y.shape[-1])
        return _silu(y)

    q = _l2norm_heads(conv(xq_ref, cwq_ref), hg, dk, dk ** -0.5)
    k = _l2norm_heads(conv(xk_ref, cwk_ref), hg, dk, 1.0)
    v = conv(xv_ref, cwv_ref)

    beta, gc, gct = _delta_gates(ba_ref[...], alog_ref[...], dtb_ref[...], t)
    lane = lax.broadcasted_iota(jnp.int32, (rows, LANES), 1)
    sub = lax.broadcasted_iota(jnp.int32, (LANES, rows), 0)
    beta_h = [jnp.sum(jnp.where(lane == hb + j, beta, 0.0), axis=1, keepdims=True) for j in js]
    gc_h = [jnp.sum(jnp.where(lane == heads + hb + j, gc, 0.0), axis=1, keepdims=True) for j in js]
    gcrow_h = [jnp.sum(jnp.where(sub == heads + hb + j, gct, 0.0), axis=0, keepdims=True) for j in js]
    glast_h = [jnp.concatenate(
        [jnp.broadcast_to(g[(b + 1) * t - 1:(b + 1) * t, :], (t, 1)) for b in range(nseq)], axis=0)
        for g in gc_h]

    ri, ci = _iotas(rows)
    same = _same_block(ri, ci, t)
    causal = same & (ri >= ci)
    strict = same & (ri > ci)
    hsl = [slice(j * dk, (j + 1) * dk) for j in js]
    us, ws, a_intras, q_decs, k_decs = _heads_prep(
        [q[:, c] for c in hsl], [k[:, c] for c in hsl], [v[:, c] for c in hsl],
        beta_h, gc_h, gcrow_h, glast_h, causal, strict, ri, ci, t, t)

    for b in range(nseq):
        r0, r1 = b * t, (b + 1) * t
        for j in js:
            wq = _mm(jnp.concatenate([ws[j][r0:r1], q_decs[j][r0:r1]], axis=0), s0_ref[b, j])
            vn_ref[j, r0:r1, :] = us[j][r0:r1] - wq[:t]
            qs_ref[j, r0:r1, :] = wq[t:]
    v_news = [vn_ref[j] for j in js]
    z = z_ref[...]
    for j in js:
        o = qs_ref[j] + _mm(a_intras[j], v_news[j])
        o_ref[:, hsl[j]] = _out_norm_gate(o, dng_ref[...], z[:, hsl[j]])

    kts = [kd.T for kd in k_decs]
    col_seq = lax.broadcasted_iota(jnp.int32, kts[0].shape, 1) >> _log2(t)
    v_new_b = [vn.astype(BF16) for vn in v_news]
    for b in range(nseq):
        for j in js:
            dl = jnp.exp(glast_h[j][b * t:b * t + 1, :])
            upd = jnp.dot(jnp.where(col_seq == b, kts[j], 0.0).astype(BF16), v_new_b[j],
                          preferred_element_type=F32)
            s_ref[b, j] = s0_ref[b, j] * dl + upd


def _delta_sample(xc, z, ba, conv_w, alog_row, dtb_row, dn_norm, s0, heads, dk, t, nseq, hg):
    rows_all = z.shape[0]
    b = rows_all // t
    ngrp = heads // hg
    kern = functools.partial(_delta_sample_kernel, nseq=nseq, t=t, heads=heads, hg=hg, dk=dk)
    xblk = (nseq * 2 * t, hg * dk)
    cwblk = (CONV_W, hg * dk)
    return pl.pallas_call(
        kern,
        grid=(b // nseq, ngrp),
        in_specs=[pl.BlockSpec(xblk, lambda i, h: (i, h)),
                  pl.BlockSpec(xblk, lambda i, h: (i, ngrp + h)),
                  pl.BlockSpec(xblk, lambda i, h: (i, 2 * ngrp + h)),
                  pl.BlockSpec(cwblk, lambda i, h: (0, h)),
                  pl.BlockSpec(cwblk, lambda i, h: (0, ngrp + h)),
                  pl.BlockSpec(cwblk, lambda i, h: (0, 2 * ngrp + h)),
                  pl.BlockSpec((nseq * t, hg * dk), lambda i, h: (i, h)),
                  pl.BlockSpec((nseq * t, LANES), lambda i, h: (i, 0)),
                  _const_spec((1, LANES)), _const_spec((1, LANES)), _const_spec((1, dk)),
                  pl.BlockSpec((nseq, hg, dk, dk), lambda i, h: (i, h, 0, 0))],
        out_specs=(pl.BlockSpec((nseq * t, hg * dk), lambda i, h: (i, h)),
                   pl.BlockSpec((nseq, hg, dk, dk), lambda i, h: (i, h, 0, 0))),
        out_shape=(jax.ShapeDtypeStruct((rows_all, heads * dk), BF16),
                   jax.ShapeDtypeStruct(s0.shape, F32)),
        scratch_shapes=[pltpu.VMEM((hg, nseq * t, dk), F32), pltpu.VMEM((hg, nseq * t, dk), F32)],
        compiler_params=_cparams(2),
        name="delta_sample",
    )(xc, xc, xc, conv_w, conv_w, conv_w, z, ba, alog_row, dtb_row, dn_norm.reshape(1, dk), s0)


def _mem_kv_kernel(m_ref, g_ref, w_ref, k_ref, v_ref):
    mn = _rms(m_ref[...], g_ref[...]).astype(BF16)
    half = k_ref.shape[1]
    k_ref[...] = jnp.dot(mn, w_ref[:, :half], preferred_element_type=F32)
    v_ref[...] = jnp.dot(mn, w_ref[:, half:], preferred_element_type=F32)


def _mem_kv(mem2d, gain, w_kv, tm):
    n, d = mem2d.shape
    half = w_kv.shape[1] // 2
    return pl.pallas_call(
        _mem_kv_kernel,
        grid=(n // tm,),
        in_specs=[pl.BlockSpec((tm, d), lambda i: (i, 0)), _const_spec((1, d)),
                  _const_spec(w_kv.shape)],
        out_specs=(pl.BlockSpec((tm, half), lambda i: (i, 0)),) * 2,
        out_shape=(jax.ShapeDtypeStruct((n, half), F32),) * 2,
        compiler_params=_cparams(1),
        name="mem_kv",
    )(mem2d, gain.reshape(1, d), w_kv)


def _attend(q, k, v, scale):
    s = _mm_nt(q, k) * scale
    e = jnp.exp(s - jnp.max(s, axis=-1, keepdims=True))
    return _mm(e, v) / jnp.sum(e, axis=-1, keepdims=True)


def _mem_attn_prompt_kernel(q_ref, k_ref, v_ref, o_ref, *, heads, hd):
    scale = hd ** -0.5
    for h in range(heads):
        sl = slice(h * hd, (h + 1) * hd)
        o_ref[0, :, sl] = _attend(q_ref[0, :, sl], k_ref[0, :, sl], v_ref[0, :, sl], scale).astype(BF16)


def _mem_attn_prompt(q, mk, mv, heads, tq):
    b, t, w = q.shape
    m = mk.shape[1]
    kern = functools.partial(_mem_attn_prompt_kernel, heads=heads, hd=w // heads)
    return pl.pallas_call(
        kern,
        grid=(b, t // tq),
        in_specs=[pl.BlockSpec((1, tq, w), lambda i, j: (i, j, 0)),
                  pl.BlockSpec((1, m, w), lambda i, j: (i, 0, 0)),
                  pl.BlockSpec((1, m, w), lambda i, j: (i, 0, 0))],
        out_specs=pl.BlockSpec((1, tq, w), lambda i, j: (i, j, 0)),
        out_shape=jax.ShapeDtypeStruct((b, t, w), BF16),
        compiler_params=_cparams(2),
        name="mem_attn_prompt",
    )(q, mk, mv)


def _mem_attn_sample_kernel(q_ref, k_ref, v_ref, o_ref, *, nseq, rows, heads, hd):
    scale = hd ** -0.5
    mh = k_ref.shape[1]
    same_head = ((lax.broadcasted_iota(jnp.int32, (rows, mh), 0) & (heads - 1))
                 == (lax.broadcasted_iota(jnp.int32, (rows, mh), 1) & (heads - 1)))
    for b in range(nseq):
        rs = slice(b * rows, (b + 1) * rows)
        s = jnp.where(same_head, _mm_nt(q_ref[rs, :], k_ref[b]) * scale, -jnp.inf)
        e = jnp.exp(s - jnp.max(s, axis=-1, keepdims=True))
        o_ref[rs, :] = _mm(e, v_ref[b]) / jnp.sum(e, axis=-1, keepdims=True)


def _mem_attn_sample(q_rows, mk, mv, heads, rows, nseq):
    n, hd = q_rows.shape
    b, mh, _ = mk.shape
    assert heads & (heads - 1) == 0
    kern = functools.partial(_mem_attn_sample_kernel, nseq=nseq, rows=rows, heads=heads, hd=hd)
    return pl.pallas_call(
        kern,
        grid=(b // nseq,),
        in_specs=[pl.BlockSpec((nseq * rows, hd), lambda i: (i, 0)),
                  pl.BlockSpec((nseq, mh, hd), lambda i: (i, 0, 0)),
                  pl.BlockSpec((nseq, mh, hd), lambda i: (i, 0, 0))],
        out_specs=pl.BlockSpec((nseq * rows, hd), lambda i: (i, 0)),
        out_shape=jax.ShapeDtypeStruct((n, hd), F32),
        compiler_params=_cparams(1),
        name="mem_attn_sample",
    )(q_rows, mk, mv)


def _merge_kernel(x_ref, odn_ref, osg_ref, omem_ref, g_ref, wg_ref, wdn_ref, wsg_ref, wmem_ref,
                  wo_ref, o_ref):
    x = x_ref[...]
    d = x.shape[-1]
    xn = _rms(x, g_ref[...]).astype(BF16)
    merged = None
    for idx, (b_ref, w_ref) in enumerate(((odn_ref, wdn_ref), (osg_ref, wsg_ref), (omem_ref, wmem_ref))):
        gate = _sigmoid(jnp.dot(xn, wg_ref[:, idx * d:(idx + 1) * d], preferred_element_type=F32))
        term = gate * _mm(b_ref[...], w_ref[...])
        merged = term if merged is None else merged + term
    o_ref[...] = x + _mm(merged, wo_ref[...])


def _merge(x2d, odn, osg, omem, norm_mix, wg, wdn, wsg, wmem, wo, tm):
    n, d = x2d.shape
    row = lambda a: pl.BlockSpec((tm, a.shape[1]), lambda i: (i, 0))
    return pl.pallas_call(
        _merge_kernel,
        grid=(n // tm,),
        in_specs=[row(x2d), row(odn), row(osg), row(omem), _const_spec((1, d)),
                  _const_spec(wg.shape), _const_spec(wdn.shape), _const_spec(wsg.shape),
                  _const_spec(wmem.shape), _const_spec(wo.shape)],
        out_specs=row(x2d),
        out_shape=jax.ShapeDtypeStruct((n, d), F32),
        compiler_params=_cparams(1),
        name="merge",
    )(x2d, odn, osg, omem, norm_mix.reshape(1, d), wg, wdn, wsg, wmem, wo)


def _ffn_kernel(x_ref, gf_ref, wgu_ref, wd_ref, gl_ref, o_ref):
    x = x_ref[...]
    dff = wd_ref.shape[0]
    hn = _rms(x, gf_ref[...]).astype(BF16)
    hg = jnp.dot(hn, wgu_ref[:, :dff], preferred_element_type=F32)
    hu = jnp.dot(hn, wgu_ref[:, dff:], preferred_element_type=F32)
    x2 = x + _mm(_silu(hg) * hu, wd_ref[...])
    o_ref[...] = _rms(x2, gl_ref[...])


def _ffn(x2d, norm_ffn, wgu, wd, norm_final, tm):
    n, d = x2d.shape
    return pl.pallas_call(
        _ffn_kernel,
        grid=(n // tm,),
        in_specs=[pl.BlockSpec((tm, d), lambda i: (i, 0)), _const_spec((1, d)),
                  _const_spec(wgu.shape), _const_spec(wd.shape), _const_spec((1, d))],
        out_specs=pl.BlockSpec((tm, d), lambda i: (i, 0)),
        out_shape=jax.ShapeDtypeStruct((n, d), F32),
        compiler_params=_cparams(1),
        name="ffn",
    )(x2d, norm_ffn.reshape(1, d), wgu, wd, norm_final.reshape(1, d))


def kernel(x_prompt, x_sample, state_delta, state_conv, cache_mem_k, cache_mem_v, mem_prompt,
           norm_mix, w_in, conv_w, a_log, dt_bias, dn_norm, sgu_norm, sgu_w, sgu_b,
           w_br_dn, w_br_sgu, w_br_mem, w_o, mem_norm, w_mem_kv, norm_ffn, w_gate_up, w_down,
           norm_final):
    depth = w_in.shape[0]
    assert depth == 1, "single-layer stack only"
    bp, tp, d = x_prompt.shape
    bs, ts, _ = x_sample.shape
    heads, dk, dv = state_delta.shape[2:]
    assert dk == dv == LANES and 2 * heads <= LANES and ts == SUBLANES
    kw = heads * dk
    qkv_w = 2 * kw + heads * dv
    sgu_groups, sgu_chunk, _ = sgu_w.shape[1:]
    sgu_width = sgu_norm.shape[-1]
    mem_tokens, mem_heads, mem_hd = cache_mem_k.shape[2:]
    mem_w = mem_heads * mem_hd
    l = 0

    splits = (qkv_w, heads * dv, heads, heads, sgu_width, sgu_width, mem_w, 3 * d)
    o = [0]
    for s in splits:
        o.append(o[-1] + s)
    w = w_in[l].astype(BF16)
    ba_pad = jnp.zeros((d, LANES - 2 * heads), w.dtype)
    assert o[2] % LANES == 0
    w_b = jnp.concatenate([w[:, o[4]:o[7]], w[:, o[2]:o[4]], ba_pad], axis=1)
    widths = (qkv_w, heads * dv, sgu_width, sgu_width, mem_w, LANES)
    w_gate = w[:, o[7]:o[8]]

    lane_pad = lambda vec: jnp.zeros((1, LANES), F32).at[0, heads:2 * heads].set(vec)
    alog_row = lane_pad(a_log[l])
    dtb_row = lane_pad(dt_bias[l])

    wdn = w_br_dn[l].astype(BF16)
    wsg = w_br_sgu[l].astype(BF16)
    wmem = w_br_mem[l].astype(BF16)
    wo = w_o[l].astype(BF16)
    wgu = w_gate_up[l].astype(BF16)
    wd = w_down[l].astype(BF16)
    wkv = w_mem_kv[l].astype(BF16)

    tm = 256

    np_ = bp * tp
    xp2 = x_prompt.reshape(np_, d)
    qkv_p, zg_p, ba_p, osg_p, qm_p, tail_p = _in_proj(
        xp2, norm_mix[l], w, w_b, sgu_norm[l], sgu_w[l], sgu_b[l].T, SGU_CHUNK, widths, tm, heads, dk,
        conv_w=conv_w[l], seq_len=tp)
    odn_p, s_p = _delta_prompt(qkv_p.reshape(bp, tp, qkv_w), zg_p.reshape(bp, tp, -1),
                               ba_p.reshape(bp, tp, LANES), alog_row, dtb_row,
                               dn_norm[l], heads, dk, 256)
    conv_p = tail_p[:, SUBLANES - (CONV_W - 1):, :]
    mk_p, mv_p = _mem_kv(mem_prompt.reshape(bp * mem_tokens, d), mem_norm[l], wkv, tm)
    mk_p = mk_p.reshape(bp, mem_tokens, mem_w)
    mv_p = mv_p.reshape(bp, mem_tokens, mem_w)
    omem_p = _mem_attn_prompt(qm_p.reshape(bp, tp, mem_w), mk_p, mv_p, mem_heads, 512)
    x1_p = _merge(xp2, odn_p.reshape(np_, -1), osg_p, omem_p.reshape(np_, mem_w), norm_mix[l],
                  w_gate, wdn, wsg, wmem, wo, tm)
    y_p = _ffn(x1_p, norm_ffn[l], wgu, wd, norm_final, tm)

    ns = bs * ts
    xs2 = x_sample.reshape(ns, d)
    reps = SGU_CHUNK // ts
    w_tiles = jnp.tile(sgu_w[l][:, :ts, :ts], (1, reps, reps))
    b_cols = jnp.tile(sgu_b[l][:, :ts], (1, reps)).T
    qkv_s, zg_s, ba_s, osg_s, qm_s, v_s = _in_proj(
        xs2, norm_mix[l], w, w_b, sgu_norm[l], w_tiles, b_cols, ts, widths, tm, heads, dk)
    qkv_s3 = qkv_s.reshape(bs, ts, qkv_w)
    hist = jnp.concatenate(
        [jnp.zeros((bs, ts - (CONV_W - 1), qkv_w), F32), state_conv[l]], axis=1)
    xc_s = jnp.concatenate([hist, qkv_s3], axis=1).reshape(bs * 2 * ts, qkv_w)
    nseq = LANES // ts
    odn_s, s_s = _delta_sample(xc_s, zg_s, ba_s, conv_w[l], alog_row, dtb_row, dn_norm[l],
                               state_delta[l], heads, dk, ts, nseq, 4)
    conv_s = qkv_s3[:, ts - (CONV_W - 1):, :]
    omem_s = _mem_attn_sample(qm_s.reshape(ns * mem_heads, mem_hd),
                              cache_mem_k[l].reshape(bs, mem_tokens * mem_heads, mem_hd),
                              cache_mem_v[l].reshape(bs, mem_tokens * mem_heads, mem_hd),
                              mem_heads, ts * mem_heads, 8).reshape(ns, mem_w)
    x1_s = _merge(xs2, odn_s, osg_s, omem_s, norm_mix[l], w_gate, wdn, wsg, wmem, wo, tm)
    y_s = _ffn(x1_s, norm_ffn[l], wgu, wd, norm_final, tm)

    return (y_p.reshape(bp, tp, d), y_s.reshape(bs, ts, d),
            s_p[None], conv_p[None],
            mk_p.reshape(1, bp, mem_tokens, mem_heads, mem_hd),
            mv_p.reshape(1, bp, mem_tokens, mem_heads, mem_hd),
            s_s[None], conv_s[None], v_s.reshape(1, bs, ts, sgu_width))
```

```python
import functools
import math

import jax
import jax.numpy as jnp
from jax import lax
from jax.experimental import pallas as pl
from jax.experimental.pallas import tpu as pltpu

F32 = jnp.float32
BF16 = jnp.bfloat16

RMS_EPS = 1e-6
L2_EPS = 1e-6
CONV_W = 4
DN_CHUNK = 64
SGU_CHUNK = 128
LANES = 128
SUBLANES = 8
VMEM_LIMIT = 56 * 1024 * 1024


def _cparams(n_axes):
    return pltpu.CompilerParams(
        dimension_semantics=("arbitrary",) * n_axes, vmem_limit_bytes=VMEM_LIMIT)


def _mm(a, b):
    return jnp.dot(a.astype(BF16), b.astype(BF16), preferred_element_type=F32)


def _mm_nt(a, b):
    return lax.dot_general(a.astype(BF16), b.astype(BF16), (((1,), (1,)), ((), ())),
                           preferred_element_type=F32)


def _mm_tn(a, b):
    return lax.dot_general(a.astype(BF16), b.astype(BF16), (((0,), (0,)), ((), ())),
                           preferred_element_type=F32)


def _rms(x, gain):
    return x * lax.rsqrt(jnp.mean(x * x, axis=-1, keepdims=True) + RMS_EPS) * gain


def _sigmoid(x):
    return 1.0 / (1.0 + jnp.exp(-x))


def _silu(x):
    return x * _sigmoid(x)


def _softplus(x):
    return jnp.maximum(x, 0.0) + jnp.log1p(jnp.exp(-jnp.abs(x)))


def _const_spec(shape):
    nd = len(shape)
    return pl.BlockSpec(shape, lambda *_: (0,) * nd)


def _l2norm_heads(a, heads, dk, scale):
    cols = []
    for h in range(heads):
        ah = a[:, h * dk:(h + 1) * dk]
        cols.append(ah * (lax.rsqrt(jnp.sum(ah * ah, axis=-1, keepdims=True) + L2_EPS) * scale))
    return jnp.concatenate(cols, axis=1)


def _in_proj_kernel(*refs, offs, heads, dk, sgu_groups, sgu_blk, seq_tiles, conv):
    x_ref, g_ref, wa_ref, wb_ref, sgn_ref, sw_ref, sb_ref = refs[:7]
    if conv:
        cw_ref, qkv_ref, zg_ref, ba_ref, osg_ref, qm_ref, tail_ref, xc_ref = refs[7:]
    else:
        qkv_ref, zg_ref, ba_ref, osg_ref, qm_ref, vsg_ref = refs[7:]
    tm = x_ref.shape[0]
    xn = _rms(x_ref[...], g_ref[...]).astype(BF16)

    def proj(k):
        w_ref, lo, hi = (wa_ref, wb_ref)[offs[k][0]], offs[k][1], offs[k][2]
        return jnp.dot(xn, w_ref[:, lo:hi], preferred_element_type=F32)

    if conv:
        hd = SUBLANES

        @pl.when(pl.program_id(0) % seq_tiles == 0)
        def _():
            xc_ref[0:hd, :] = jnp.zeros((hd, xc_ref.shape[1]), F32)

        xc_ref[hd:hd + tm, :] = proj(0)
        cw = cw_ref[...]
        y = xc_ref[pl.ds(hd - (CONV_W - 1), tm), :] * cw[0:1, :]
        for i in range(1, CONV_W):
            y = y + xc_ref[pl.ds(hd - (CONV_W - 1) + i, tm), :] * cw[i:i + 1, :]
        last = xc_ref[tm:tm + hd, :]
        tail_ref[0] = last
        xc_ref[0:hd, :] = last
        act = _silu(y)
        kw = heads * dk
        qkv_ref[:, 0:kw] = _l2norm_heads(act[:, 0:kw], heads, dk, dk ** -0.5).astype(BF16)
        qkv_ref[:, kw:2 * kw] = _l2norm_heads(act[:, kw:2 * kw], heads, dk, 1.0).astype(BF16)
        qkv_ref[:, 2 * kw:] = act[:, 2 * kw:].astype(BF16)
    else:
        qkv_ref[...] = proj(0)
    zg_ref[...] = _silu(proj(1)).astype(BF16)
    qm_ref[...] = proj(4).astype(qm_ref.dtype)
    ba_ref[...] = proj(5)

    u = jax.nn.gelu(proj(2))
    v = _rms(jax.nn.gelu(proj(3)), sgn_ref[...])
    if not conv:
        vsg_ref[...] = v
    rows = sw_ref.shape[1]
    ri, ci = _iotas(rows)
    causal = _same_block(ri, ci, sgu_blk) & (ri >= ci)
    bias = sb_ref[...]
    gcw = u.shape[1] // sgu_groups
    vb = v.astype(BF16)
    for g in range(sgu_groups):
        wc = jnp.where(causal, sw_ref[g], 0.0).astype(BF16)
        cs = slice(g * gcw, (g + 1) * gcw)
        for rb in range(tm // rows):
            rs = slice(rb * rows, (rb + 1) * rows)
            mixed = jnp.dot(wc, vb[rs, cs], preferred_element_type=F32) + bias[:, g:g + 1]
            osg_ref[rs, cs] = (u[rs, cs] * mixed).astype(BF16)


def _in_proj(x2d, norm_mix, w_all, w_b, sgu_norm, sgu_w, sgu_bcols, sgu_blk, widths, tm, heads, dk,
             conv_w=None, seq_len=None):
    n, d = x2d.shape
    conv = conv_w is not None
    qkv_w, z_w, sg_w, _, qm_w, ba_w = widths
    offs = ((0, 0, qkv_w), (0, qkv_w, qkv_w + z_w))
    lo = 0
    for w_ in widths[2:]:
        offs += ((1, lo, lo + w_),)
        lo += w_
    assert lo == w_b.shape[1]
    row = lambda w_: pl.BlockSpec((tm, w_), lambda i: (i, 0))
    out_shape = [jax.ShapeDtypeStruct((n, qkv_w), BF16 if conv else F32),
                 jax.ShapeDtypeStruct((n, z_w), BF16),
                 jax.ShapeDtypeStruct((n, ba_w), F32),
                 jax.ShapeDtypeStruct((n, sg_w), BF16),
                 jax.ShapeDtypeStruct((n, qm_w), BF16 if conv else F32)]
    out_specs = [row(qkv_w), row(z_w), row(ba_w), row(sg_w), row(qm_w)]
    in_specs = [row(d), _const_spec((1, d)), _const_spec((d, qkv_w + z_w)), _const_spec(w_b.shape),
                _const_spec((1, sg_w)), _const_spec(sgu_w.shape), _const_spec(sgu_bcols.shape)]
    args = [x2d, norm_mix.reshape(1, d), w_all, w_b, sgu_norm.reshape(1, -1), sgu_w, sgu_bcols]
    scratch = []
    seq_tiles = 1
    if conv:
        seq_tiles = seq_len // tm
        in_specs.append(_const_spec(conv_w.shape))
        args.append(conv_w)
        out_shape.append(jax.ShapeDtypeStruct((n // seq_len, SUBLANES, qkv_w), F32))
        out_specs.append(pl.BlockSpec((1, SUBLANES, qkv_w), lambda i: (i // seq_tiles, 0, 0)))
        scratch.append(pltpu.VMEM((tm + 2 * SUBLANES, qkv_w), F32))
    else:
        out_shape.append(jax.ShapeDtypeStruct((n, sg_w), F32))
        out_specs.append(row(sg_w))
    kern = functools.partial(_in_proj_kernel, offs=offs, heads=heads, dk=dk,
                             sgu_groups=sgu_w.shape[0], sgu_blk=sgu_blk, seq_tiles=seq_tiles, conv=conv)
    return pl.pallas_call(
        kern,
        grid=(n // tm,),
        in_specs=in_specs,
        out_specs=tuple(out_specs),
        out_shape=tuple(out_shape),
        scratch_shapes=scratch,
        compiler_params=_cparams(1),
        name="in_proj_prompt" if conv else "in_proj_sample",
    )(*args)


def _log2(v):
    r = int(math.log2(v))
    assert 1 << r == v
    return r


def _iotas(n):
    ri = lax.broadcasted_iota(jnp.int32, (n, n), 0)
    ci = lax.broadcasted_iota(jnp.int32, (n, n), 1)
    return ri, ci


def _same_block(ri, ci, size):
    s = _log2(size)
    return (ri >> s) == (ci >> s)


def _unit_lower_inverse(lmats, ri, ci, blk, base):
    eye = (ri == ci).astype(F32)
    if base < blk:
        same = _same_block(ri, ci, base)
        lbs = [jnp.where(same, lm, 0.0) for lm in lmats]
    else:
        lbs = list(lmats)
    ps = lbs
    ts = [eye - lb for lb in lbs]
    k = 2
    while k < base:
        ps = [_mm(p, p) for p in ps]
        ts = [t + _mm(t, p) for t, p in zip(ts, ps)]
        k *= 2
    s = base
    while s < blk:
        sh = _log2(s)
        off = _same_block(ri, ci, 2 * s) & (((ri >> sh) & 1) == 1) & (((ci >> sh) & 1) == 0)
        xs = [_mm(jnp.where(off, lm, 0.0), t) for lm, t in zip(lmats, ts)]
        ts = [t - _mm(t, x) for t, x in zip(ts, xs)]
        s *= 2
    return ts


def _chunk_cumsum(g, chunk):
    n = g.shape[0]
    row = lax.broadcasted_iota(jnp.int32, g.shape, 0) & (chunk - 1)
    s = 1
    while s < chunk:
        shifted = pltpu.roll(g, s, 0)
        g = g + jnp.where(row >= s, shifted, 0.0)
        s *= 2
    return g


def _delta_gates(ba, alog_row, dtb_row, chunk):
    beta = _sigmoid(ba)
    g = -jnp.exp(alog_row) * _softplus(ba + dtb_row)
    gc = _chunk_cumsum(g, chunk)
    return beta, gc, gc.T


def _heads_prep(qs, ks, vs, betas, gcs, gcrows, glasts, causal, strict, ri, ci, blk, base):
    dv = vs[0].shape[-1]
    egs = [jnp.exp(gc) for gc in gcs]
    kbs = [k * b for k, b in zip(ks, betas)]
    vbs = [v * b for v, b in zip(vs, betas)]
    decays = [jnp.exp(jnp.where(causal, gc - gr, -jnp.inf)) for gc, gr in zip(gcs, gcrows)]
    lmats = [jnp.where(strict, _mm_nt(kb, k) * dec, 0.0) for kb, k, dec in zip(kbs, ks, decays)]
    a_intras = [_mm_nt(q, k) * dec for q, k, dec in zip(qs, ks, decays)]
    tmats = _unit_lower_inverse(lmats, ri, ci, blk, base)
    uws = [_mm(t, jnp.concatenate([vb, kb * eg], axis=1))
           for t, vb, kb, eg in zip(tmats, vbs, kbs, egs)]
    us = [uw[:, :dv] for uw in uws]
    ws = [uw[:, dv:] for uw in uws]
    q_decs = [q * eg for q, eg in zip(qs, egs)]
    k_decs = [k * jnp.exp(gl - gc) for k, gl, gc in zip(ks, glasts, gcs)]
    return us, ws, a_intras, q_decs, k_decs


def _out_norm_gate(o, dn_gain, zgate):
    return (_rms(o, dn_gain) * zgate).astype(BF16)


def _delta_prompt_kernel(qkv_ref, z_ref, ba_ref, alog_ref, dtb_ref, dng_ref,
                         o_ref, s_ref, *, tb, heads, dk):
    @pl.when(pl.program_id(1) == 0)
    def _():
        s_ref[...] = jnp.zeros_like(s_ref)

    qkv = qkv_ref[0]
    beta, gc, gct = _delta_gates(ba_ref[0], alog_ref[...], dtb_ref[...], DN_CHUNK)
    nchunk = tb // DN_CHUNK
    glast = jnp.concatenate(
        [jnp.broadcast_to(gc[(c + 1) * DN_CHUNK - 1:(c + 1) * DN_CHUNK, :], (DN_CHUNK, gc.shape[1]))
         for c in range(nchunk)], axis=0)

    ri, ci = _iotas(tb)
    same = _same_block(ri, ci, DN_CHUNK)
    causal = same & (ri >= ci)
    strict = same & (ri > ci)
    kw = heads * dk
    z = z_ref[0]
    dng = dng_ref[...]

    hs = range(heads)
    gcols = [heads + h for h in hs]
    us, ws, a_intras, q_decs, k_decs = _heads_prep(
        [qkv[:, h * dk:(h + 1) * dk].astype(F32) for h in hs],
        [qkv[:, kw + h * dk:kw + (h + 1) * dk].astype(F32) for h in hs],
        [qkv[:, 2 * kw + h * dk:2 * kw + (h + 1) * dk].astype(F32) for h in hs],
        [beta[:, h:h + 1] for h in hs],
        [gc[:, g:g + 1] for g in gcols],
        [gct[g:g + 1, :] for g in gcols],
        [glast[:, g:g + 1] for g in gcols],
        causal, strict, ri, ci, DN_CHUNK, 16)
    states = [s_ref[0, h] for h in hs]
    o_inter = [[] for _ in hs]
    v_news = [[] for _ in hs]
    for c in range(nchunk):
        r0, r1 = c * DN_CHUNK, (c + 1) * DN_CHUNK
        for h in hs:
            s = states[h]
            wq = _mm(jnp.concatenate([ws[h][r0:r1], q_decs[h][r0:r1]], axis=0), s)
            v_new = us[h][r0:r1] - wq[:DN_CHUNK]
            v_news[h].append(v_new)
            o_inter[h].append(wq[DN_CHUNK:])
            dl = jnp.exp(glast[r0:r0 + 1, gcols[h]:gcols[h] + 1])
            states[h] = s * dl + _mm_tn(k_decs[h][r0:r1], v_new)
    for h in hs:
        s_ref[0, h] = states[h]
        o = jnp.concatenate(o_inter[h], axis=0) + _mm(a_intras[h], jnp.concatenate(v_news[h], axis=0))
        o_ref[0, :, h * dk:(h + 1) * dk] = _out_norm_gate(o, dng, z[:, h * dk:(h + 1) * dk])


def _delta_prompt(qkv, zg, ba, alog_row, dtb_row, dn_norm, heads, dk, tb):
    b, t, c3 = qkv.shape
    vw = zg.shape[-1]
    kern = functools.partial(_delta_prompt_kernel, tb=tb, heads=heads, dk=dk)
    return pl.pallas_call(
        kern,
        grid=(b, t // tb),
        in_specs=[pl.BlockSpec((1, tb, c3), lambda i, j: (i, j, 0)),
                  pl.BlockSpec((1, tb, vw), lambda i, j: (i, j, 0)),
                  pl.BlockSpec((1, tb, LANES), lambda i, j: (i, j, 0)),
                  _const_spec((1, LANES)), _const_spec((1, LANES)), _const_spec((1, dk))],
        out_specs=(pl.BlockSpec((1, tb, vw), lambda i, j: (i, j, 0)),
                   pl.BlockSpec((1, heads, dk, dk), lambda i, j: (i, 0, 0, 0))),
        out_shape=(jax.ShapeDtypeStruct((b, t, vw), BF16),
                   jax.ShapeDtypeStruct((b, heads, dk, dk), F32)),
        compiler_params=_cparams(2),
        name="delta_prompt",
    )(qkv, zg, ba, alog_row, dtb_row, dn_norm.reshape(1, dk))


def _delta_sample_kernel(xq_ref, xk_ref, xv_ref, cwq_ref, cwk_ref, cwv_ref, z_ref, ba_ref,
                         alog_ref, dtb_ref, dng_ref, s0_ref, o_ref, s_ref, vn_ref, qs_ref,
                         *, nseq, t, heads, hg, dk):
    hb = pl.program_id(1) * hg
    rows = nseq * t
    grp = 2 * t
    js = range(hg)

    def conv(x_ref, cw_ref):
        x = x_ref[...]
        cw = cw_ref[...]
        y = x * cw[CONV_W - 1:CONV_W, :]
        for i in range(CONV_W - 1):
            y = y + pltpu.roll(x, CONV_W - 1 - i, 0) * cw[i:i + 1, :]
        y = y.reshape(nseq, grp, y.shape[-1])[:, t:, :].reshape(rows, y.shape[-1])
        return _silu(y)

    q = _l2norm_heads(conv(xq_ref, cwq_ref), hg, dk, dk ** -0.5)
    k = _l2norm_heads(conv(xk_ref, cwk_ref), hg, dk, 1.0)
    v = conv(xv_ref, cwv_ref)

    beta, gc, gct = _delta_gates(ba_ref[...], alog_ref[...], dtb_ref[...], t)
    lane = lax.broadcasted_iota(jnp.int32, (rows, LANES), 1)
    sub = lax.broadcasted_iota(jnp.int32, (LANES, rows), 0)
    beta_h = [jnp.sum(jnp.where(lane == hb + j, beta, 0.0), axis=1, keepdims=True) for j in js]
    gc_h = [jnp.sum(jnp.where(lane == heads + hb + j, gc, 0.0), axis=1, keepdims=True) for j in js]
    gcrow_h = [jnp.sum(jnp.where(sub == heads + hb + j, gct, 0.0), axis=0, keepdims=True) for j in js]
    glast_h = [jnp.concatenate(
        [jnp.broadcast_to(g[(b + 1) * t - 1:(b + 1) * t, :], (t, 1)) for b in range(nseq)], axis=0)
        for g in gc_h]

    ri, ci = _iotas(rows)
    same = _same_block(ri, ci, t)
    causal = same & (ri >= ci)
    strict = same & (ri > ci)
    hsl = [slice(j * dk, (j + 1) * dk) for j in js]
    us, ws, a_intras, q_decs, k_decs = _heads_prep(
        [q[:, c] for c in hsl], [k[:, c] for c in hsl], [v[:, c] for c in hsl],
        beta_h, gc_h, gcrow_h, glast_h, causal, strict, ri, ci, t, t)

    for b in range(nseq):
        r0, r1 = b * t, (b + 1) * t
        for j in js:
            wq = _mm(jnp.concatenate([ws[j][r0:r1], q_decs[j][r0:r1]], axis=0), s0_ref[b, j])
            vn_ref[j, r0:r1, :] = us[j][r0:r1] - wq[:t]
            qs_ref[j, r0:r1, :] = wq[t:]
    v_news = [vn_ref[j] for j in js]
    z = z_ref[...]
    for j in js:
        o = qs_ref[j] + _mm(a_intras[j], v_news[j])
        o_ref[:, hsl[j]] = _out_norm_gate(o, dng_ref[...], z[:, hsl[j]])

    kts = [kd.T for kd in k_decs]
    col_seq = lax.broadcasted_iota(jnp.int32, kts[0].shape, 1) >> _log2(t)
    v_new_b = [vn.astype(BF16) for vn in v_news]
    for b in range(nseq):
        for j in js:
            dl = jnp.exp(glast_h[j][b * t:b * t + 1, :])
            upd = jnp.dot(jnp.where(col_seq == b, kts[j], 0.0).astype(BF16), v_new_b[j],
                          preferred_element_type=F32)
            s_ref[b, j] = s0_ref[b, j] * dl + upd


def _delta_sample(xc, z, ba, conv_w, alog_row, dtb_row, dn_norm, s0, heads, dk, t, nseq, hg):
    rows_all = z.shape[0]
    b = rows_all // t
    ngrp = heads // hg
    kern = functools.partial(_delta_sample_kernel, nseq=nseq, t=t, heads=heads, hg=hg, dk=dk)
    xblk = (nseq * 2 * t, hg * dk)
    cwblk = (CONV_W, hg * dk)
    return pl.pallas_call(
        kern,
        grid=(b // nseq, ngrp),
        in_specs=[pl.BlockSpec(xblk, lambda i, h: (i, h)),
                  pl.BlockSpec(xblk, lambda i, h: (i, ngrp + h)),
                  pl.BlockSpec(xblk, lambda i, h: (i, 2 * ngrp + h)),
                  pl.BlockSpec(cwblk, lambda i, h: (0, h)),
                  pl.BlockSpec(cwblk, lambda i, h: (0, ngrp + h)),
                  pl.BlockSpec(cwblk, lambda i, h: (0, 2 * ngrp + h)),
                  pl.BlockSpec((nseq * t, hg * dk), lambda i, h: (i, h)),
                  pl.BlockSpec((nseq * t, LANES), lambda i, h: (i, 0)),
                  _const_spec((1, LANES)), _const_spec((1, LANES)), _const_spec((1, dk)),
                  pl.BlockSpec((nseq, hg, dk, dk), lambda i, h: (i, h, 0, 0))],
        out_specs=(pl.BlockSpec((nseq * t, hg * dk), lambda i, h: (i, h)),
                   pl.BlockSpec((nseq, hg, dk, dk), lambda i, h: (i, h, 0, 0))),
        out_shape=(jax.ShapeDtypeStruct((rows_all, heads * dk), BF16),
                   jax.ShapeDtypeStruct(s0.shape, F32)),
        scratch_shapes=[pltpu.VMEM((hg, nseq * t, dk), F32), pltpu.VMEM((hg, nseq * t, dk), F32)],
        compiler_params=_cparams(2),
        name="delta_sample",
    )(xc, xc, xc, conv_w, conv_w, conv_w, z, ba, alog_row, dtb_row, dn_norm.reshape(1, dk), s0)


def _mem_kv_kernel(m_ref, g_ref, w_ref, k_ref, v_ref):
    mn = _rms(m_ref[...], g_ref[...]).astype(BF16)
    half = k_ref.shape[1]
    k_ref[...] = jnp.dot(mn, w_ref[:, :half], preferred_element_type=F32)
    v_ref[...] = jnp.dot(mn, w_ref[:, half:], preferred_element_type=F32)


def _mem_kv(mem2d, gain, w_kv, tm):
    n, d = mem2d.shape
    half = w_kv.shape[1] // 2
    return pl.pallas_call(
        _mem_kv_kernel,
        grid=(n // tm,),
        in_specs=[pl.BlockSpec((tm, d), lambda i: (i, 0)), _const_spec((1, d)),
                  _const_spec(w_kv.shape)],
        out_specs=(pl.BlockSpec((tm, half), lambda i: (i, 0)),) * 2,
        out_shape=(jax.ShapeDtypeStruct((n, half), F32),) * 2,
        compiler_params=_cparams(1),
        name="mem_kv",
    )(mem2d, gain.reshape(1, d), w_kv)


def _attend(q, k, v, scale):
    s = _mm_nt(q, k) * scale
    e = jnp.exp(s - jnp.max(s, axis=-1, keepdims=True))
    return _mm(e, v) / jnp.sum(e, axis=-1, keepdims=True)


def _mem_attn_prompt_kernel(q_ref, k_ref, v_ref, o_ref, *, heads, hd):
    scale = hd ** -0.5
    for h in range(heads):
        sl = slice(h * hd, (h + 1) * hd)
        o_ref[0, :, sl] = _attend(q_ref[0, :, sl], k_ref[0, :, sl], v_ref[0, :, sl], scale).astype(BF16)


def _mem_attn_prompt(q, mk, mv, heads, tq):
    b, t, w = q.shape
    m = mk.shape[1]
    kern = functools.partial(_mem_attn_prompt_kernel, heads=heads, hd=w // heads)
    return pl.pallas_call(
        kern,
        grid=(b, t // tq),
        in_specs=[pl.BlockSpec((1, tq, w), lambda i, j: (i, j, 0)),
                  pl.BlockSpec((1, m, w), lambda i, j: (i, 0, 0)),
                  pl.BlockSpec((1, m, w), lambda i, j: (i, 0, 0))],
        out_specs=pl.BlockSpec((1, tq, w), lambda i, j: (i, j, 0)),
        out_shape=jax.ShapeDtypeStruct((b, t, w), BF16),
        compiler_params=_cparams(2),
        name="mem_attn_prompt",
    )(q, mk, mv)


def _mem_attn_sample_kernel(q_ref, k_ref, v_ref, o_ref, *, nseq, rows, heads, hd):
    scale = hd ** -0.5
    mh = k_ref.shape[1]
    same_head = ((lax.broadcasted_iota(jnp.int32, (rows, mh), 0) & (heads - 1))
                 == (lax.broadcasted_iota(jnp.int32, (rows, mh), 1) & (heads - 1)))
    for b in range(nseq):
        rs = slice(b * rows, (b + 1) * rows)
        s = jnp.where(same_head, _mm_nt(q_ref[rs, :], k_ref[b]) * scale, -jnp.inf)
        e = jnp.exp(s - jnp.max(s, axis=-1, keepdims=True))
        o_ref[rs, :] = _mm(e, v_ref[b]) / jnp.sum(e, axis=-1, keepdims=True)


def _mem_attn_sample(q_rows, mk, mv, heads, rows, nseq):
    n, hd = q_rows.shape
    b, mh, _ = mk.shape
    assert heads & (heads - 1) == 0
    kern = functools.partial(_mem_attn_sample_kernel, nseq=nseq, rows=rows, heads=heads, hd=hd)
    return pl.pallas_call(
        kern,
        grid=(b // nseq,),
        in_specs=[pl.BlockSpec((nseq * rows, hd), lambda i: (i, 0)),
                  pl.BlockSpec((nseq, mh, hd), lambda i: (i, 0, 0)),
                  pl.BlockSpec((nseq, mh, hd), lambda i: (i, 0, 0))],
        out_specs=pl.BlockSpec((nseq * rows, hd), lambda i: (i, 0)),
        out_shape=jax.ShapeDtypeStruct((n, hd), F32),
        compiler_params=_cparams(1),
        name="mem_attn_sample",
    )(q_rows, mk, mv)


def _merge_kernel(x_ref, odn_ref, osg_ref, omem_ref, g_ref, wg_ref, wdn_ref, wsg_ref, wmem_ref,
                  wo_ref, o_ref):
    x = x_ref[...]
    d = x.shape[-1]
    xn = _rms(x, g_ref[...]).astype(BF16)
    merged = None
    for idx, (b_ref, w_ref) in enumerate(((odn_ref, wdn_ref), (osg_ref, wsg_ref), (omem_ref, wmem_ref))):
        gate = _sigmoid(jnp.dot(xn, wg_ref[:, idx * d:(idx + 1) * d], preferred_element_type=F32))
        term = gate * _mm(b_ref[...], w_ref[...])
        merged = term if merged is None else merged + term
    o_ref[...] = x + _mm(merged, wo_ref[...])


def _merge(x2d, odn, osg, omem, norm_mix, wg, wdn, wsg, wmem, wo, tm):
    n, d = x2d.shape
    row = lambda a: pl.BlockSpec((tm, a.shape[1]), lambda i: (i, 0))
    return pl.pallas_call(
        _merge_kernel,
        grid=(n // tm,),
        in_specs=[row(x2d), row(odn), row(osg), row(omem), _const_spec((1, d)),
                  _const_spec(wg.shape), _const_spec(wdn.shape), _const_spec(wsg.shape),
                  _const_spec(wmem.shape), _const_spec(wo.shape)],
        out_specs=row(x2d),
        out_shape=jax.ShapeDtypeStruct((n, d), F32),
        compiler_params=_cparams(1),
        name="merge",
    )(x2d, odn, osg, omem, norm_mix.reshape(1, d), wg, wdn, wsg, wmem, wo)


def _ffn_kernel(x_ref, gf_ref, wgu_ref, wd_ref, gl_ref, o_ref):
    x = x_ref[...]
    dff = wd_ref.shape[0]
    hn = _rms(x, gf_ref[...]).astype(BF16)
    hg = jnp.dot(hn, wgu_ref[:, :dff], preferred_element_type=F32)
    hu = jnp.dot(hn, wgu_ref[:, dff:], preferred_element_type=F32)
    x2 = x + _mm(_silu(hg) * hu, wd_ref[...])
    o_ref[...] = _rms(x2, gl_ref[...])


def _ffn(x2d, norm_ffn, wgu, wd, norm_final, tm):
    n, d = x2d.shape
    return pl.pallas_call(
        _ffn_kernel,
        grid=(n // tm,),
        in_specs=[pl.BlockSpec((tm, d), lambda i: (i, 0)), _const_spec((1, d)),
                  _const_spec(wgu.shape), _const_spec(wd.shape), _const_spec((1, d))],
        out_specs=pl.BlockSpec((tm, d), lambda i: (i, 0)),
        out_shape=jax.ShapeDtypeStruct((n, d), F32),
        compiler_params=_cparams(1),
        name="ffn",
    )(x2d, norm_ffn.reshape(1, d), wgu, wd, norm_final.reshape(1, d))


def kernel(x_prompt, x_sample, state_delta, state_conv, cache_mem_k, cache_mem_v, mem_prompt,
           norm_mix, w_in, conv_w, a_log, dt_bias, dn_norm, sgu_norm, sgu_w, sgu_b,
           w_br_dn, w_br_sgu, w_br_mem, w_o, mem_norm, w_mem_kv, norm_ffn, w_gate_up, w_down,
           norm_final):
    depth = w_in.shape[0]
    assert depth == 1, "single-layer stack only"
    bp, tp, d = x_prompt.shape
    bs, ts, _ = x_sample.shape
    heads, dk, dv = state_delta.shape[2:]
    assert dk == dv == LANES and 2 * heads <= LANES and ts == SUBLANES
    kw = heads * dk
    qkv_w = 2 * kw + heads * dv
    sgu_groups, sgu_chunk, _ = sgu_w.shape[1:]
    sgu_width = sgu_norm.shape[-1]
    mem_tokens, mem_heads, mem_hd = cache_mem_k.shape[2:]
    mem_w = mem_heads * mem_hd
    l = 0

    splits = (qkv_w, heads * dv, heads, heads, sgu_width, sgu_width, mem_w, 3 * d)
    o = [0]
    for s in splits:
        o.append(o[-1] + s)
    w = w_in[l].astype(BF16)
    ba_pad = jnp.zeros((d, LANES - 2 * heads), w.dtype)
    assert o[2] % LANES == 0
    w_b = jnp.concatenate([w[:, o[4]:o[7]], w[:, o[2]:o[4]], ba_pad], axis=1)
    widths = (qkv_w, heads * dv, sgu_width, sgu_width, mem_w, LANES)
    w_gate = w[:, o[7]:o[8]]

    lane_pad = lambda vec: jnp.zeros((1, LANES), F32).at[0, heads:2 * heads].set(vec)
    alog_row = lane_pad(a_log[l])
    dtb_row = lane_pad(dt_bias[l])

    wdn = w_br_dn[l].astype(BF16)
    wsg = w_br_sgu[l].astype(BF16)
    wmem = w_br_mem[l].astype(BF16)
    wo = w_o[l].astype(BF16)
    wgu = w_gate_up[l].astype(BF16)
    wd = w_down[l].astype(BF16)
    wkv = w_mem_kv[l].astype(BF16)

    tm = 256

    np_ = bp * tp
    xp2 = x_prompt.reshape(np_, d)
    qkv_p, zg_p, ba_p, osg_p, qm_p, tail_p = _in_proj(
        xp2, norm_mix[l], w, w_b, sgu_norm[l], sgu_w[l], sgu_b[l].T, SGU_CHUNK, widths, tm, heads, dk,
        conv_w=conv_w[l], seq_len=tp)
    odn_p, s_p = _delta_prompt(qkv_p.reshape(bp, tp, qkv_w), zg_p.reshape(bp, tp, -1),
                               ba_p.reshape(bp, tp, LANES), alog_row, dtb_row,
                               dn_norm[l], heads, dk, 256)
    conv_p = tail_p[:, SUBLANES - (CONV_W - 1):, :]
    mk_p, mv_p = _mem_kv(mem_prompt.reshape(bp * mem_tokens, d), mem_norm[l], wkv, tm)
    mk_p = mk_p.reshape(bp, mem_tokens, mem_w)
    mv_p = mv_p.reshape(bp, mem_tokens, mem_w)
    omem_p = _mem_attn_prompt(qm_p.reshape(bp, tp, mem_w), mk_p, mv_p, mem_heads, 512)
    x1_p = _merge(xp2, odn_p.reshape(np_, -1), osg_p, omem_p.reshape(np_, mem_w), norm_mix[l],
                  w_gate, wdn, wsg, wmem, wo, tm)
    y_p = _ffn(x1_p, norm_ffn[l], wgu, wd, norm_final, tm)

    ns = bs * ts
    xs2 = x_sample.reshape(ns, d)
    reps = SGU_CHUNK // ts
    w_tiles = jnp.tile(sgu_w[l][:, :ts, :ts], (1, reps, reps))
    b_cols = jnp.tile(sgu_b[l][:, :ts], (1, reps)).T
    qkv_s, zg_s, ba_s, osg_s, qm_s, v_s = _in_proj(
        xs2, norm_mix[l], w, w_b, sgu_norm[l], w_tiles, b_cols, ts, widths, tm, heads, dk)
    qkv_s3 = qkv_s.reshape(bs, ts, qkv_w)
    hist = jnp.concatenate(
        [jnp.zeros((bs, ts - (CONV_W - 1), qkv_w), F32), state_conv[l]], axis=1)
    xc_s = jnp.concatenate([hist, qkv_s3], axis=1).reshape(bs * 2 * ts, qkv_w)
    nseq = LANES // ts
    odn_s, s_s = _delta_sample(xc_s, zg_s, ba_s, conv_w[l], alog_row, dtb_row, dn_norm[l],
                               state_delta[l], heads, dk, ts, nseq, 4)
    conv_s = qkv_s3[:, ts - (CONV_W - 1):, :]
    omem_s = _mem_attn_sample(qm_s.reshape(ns * mem_heads, mem_hd),
                              cache_mem_k[l].reshape(bs, mem_tokens * mem_heads, mem_hd),
                              cache_mem_v[l].reshape(bs, mem_tokens * mem_heads, mem_hd),
                              mem_heads, ts * mem_heads, 8).reshape(ns, mem_w)
    x1_s = _merge(xs2, odn_s, osg_s, omem_s, norm_mix[l], w_gate, wdn, wsg, wmem, wo, tm)
    y_s = _ffn(x1_s, norm_ffn[l], wgu, wd, norm_final, tm)

    return (y_p.reshape(bp, tp, d), y_s.reshape(bs, ts, d),
            s_p[None], conv_p[None],
            mk_p.reshape(1, bp, mem_tokens, mem_heads, mem_hd),
            mv_p.reshape(1, bp, mem_tokens, mem_heads, mem_hd),
            s_s[None], conv_s[None], v_s.reshape(1, bs, ts, sgu_width))
```

```python
import functools
import math

import jax
import jax.numpy as jnp
from jax import lax
from jax.experimental import pallas as pl
from jax.experimental.pallas import tpu as pltpu

F32 = jnp.float32
BF16 = jnp.bfloat16

RMS_EPS = 1e-6
L2_EPS = 1e-6
CONV_W = 4
DN_CHUNK = 64
SGU_CHUNK = 128
CONV_COLS = 512
LANES = 128
SUBLANES = 8
VMEM_LIMIT = 56 * 1024 * 1024


def _cparams(n_axes):
    return pltpu.CompilerParams(
        dimension_semantics=("arbitrary",) * n_axes, vmem_limit_bytes=VMEM_LIMIT)


def _mm(a, b):
    return jnp.dot(a.astype(BF16), b.astype(BF16), preferred_element_type=F32)


def _mm_nt(a, b):
    return lax.dot_general(a.astype(BF16), b.astype(BF16), (((1,), (1,)), ((), ())),
                           preferred_element_type=F32)


def _mm_tn(a, b):
    return lax.dot_general(a.astype(BF16), b.astype(BF16), (((0,), (0,)), ((), ())),
                           preferred_element_type=F32)


def _rms(x, gain):
    return x * lax.rsqrt(jnp.mean(x * x, axis=-1, keepdims=True) + RMS_EPS) * gain


NEG_LOG2E = -1.4426950408889634


def _sigmoid(x):
    return 1.0 / (1.0 + jnp.exp2(x * NEG_LOG2E))


def _silu(x):
    return x * _sigmoid(x)


def _softplus(x):
    return jnp.maximum(x, 0.0) + jnp.log1p(jnp.exp(-jnp.abs(x)))


def _const_spec(shape):
    nd = len(shape)
    return pl.BlockSpec(shape, lambda *_: (0,) * nd, pipeline_mode=pl.Buffered(1))


def _l2norm_heads(a, heads, dk, scale):
    cols = []
    for h in range(heads):
        ah = a[:, h * dk:(h + 1) * dk]
        cols.append(ah * (lax.rsqrt(jnp.sum(ah * ah, axis=-1, keepdims=True) + L2_EPS) * scale))
    return jnp.concatenate(cols, axis=1)


def _in_proj_kernel(*refs, offs, heads, dk, sgu_groups, sgu_blk, seq_tiles, conv):
    x_ref, g_ref, wa_ref, wb_ref, sgn_ref, sw_ref, sb_ref = refs[:7]
    if conv:
        cw_ref, qkv_ref, zg_ref, ba_ref, osg_ref, qm_ref, tail_ref, xc_ref = refs[7:]
    else:
        qkv_ref, zg_ref, ba_ref, osg_ref, qm_ref, vsg_ref = refs[7:]
    tm = x_ref.shape[0]
    xn = _rms(x_ref[...], g_ref[...]).astype(BF16)

    def proj(k):
        w_ref, lo, hi = (wa_ref, wb_ref)[offs[k][0]], offs[k][1], offs[k][2]
        return jnp.dot(xn, w_ref[:, lo:hi], preferred_element_type=F32)

    if conv:
        hd = SUBLANES

        @pl.when(pl.program_id(0) % seq_tiles == 0)
        def _():
            xc_ref[...] = jnp.zeros(xc_ref.shape, F32)

        cw = cw_ref[...]
        for c0 in range(0, offs[0][2], CONV_COLS):
            cs = slice(c0, c0 + CONV_COLS)
            raw = jnp.dot(xn, wa_ref[:, cs], preferred_element_type=F32)
            ext = jnp.concatenate([xc_ref[:, cs], raw], axis=0)
            y = raw * cw[CONV_W - 1:CONV_W, cs]
            for i in range(CONV_W - 1):
                y = y + pltpu.roll(ext, CONV_W - 1 - i, 0)[hd:] * cw[i:i + 1, cs]
            last = raw[tm - hd:]
            tail_ref[0, :, cs] = last
            xc_ref[:, cs] = last
            act = _silu(y)
            if c0 < 2 * heads * dk:
                act = _l2norm_heads(act, CONV_COLS // dk, dk, dk ** -0.5 if c0 < heads * dk else 1.0)
            qkv_ref[:, cs] = act.astype(BF16)
    else:
        qkv_ref[...] = proj(0)
    u = jax.nn.gelu(proj(2))
    v = _rms(jax.nn.gelu(proj(3)), sgn_ref[...])
    if not conv:
        vsg_ref[...] = v
    rows = sw_ref.shape[1]
    ri, ci = _iotas(rows)
    causal = _same_block(ri, ci, sgu_blk) & (ri >= ci)
    bias = sb_ref[...]
    gcw = u.shape[1] // sgu_groups
    vb = v.astype(BF16)
    for g in range(sgu_groups):
        wc = jnp.where(causal, sw_ref[g], 0.0).astype(BF16)
        cs = slice(g * gcw, (g + 1) * gcw)
        for rb in range(tm // rows):
            rs = slice(rb * rows, (rb + 1) * rows)
            mixed = jnp.dot(wc, vb[rs, cs], preferred_element_type=F32) + bias[:, g:g + 1]
            osg_ref[rs, cs] = (u[rs, cs] * mixed).astype(BF16)

    zg_ref[...] = _silu(proj(1)).astype(BF16)
    qm_ref[...] = proj(4).astype(qm_ref.dtype)
    ba_ref[...] = proj(5)


def _in_proj(x2d, norm_mix, w_all, w_b, sgu_norm, sgu_w, sgu_bcols, sgu_blk, widths, tm, heads, dk,
             conv_w=None, seq_len=None):
    n, d = x2d.shape
    conv = conv_w is not None
    qkv_w, z_w, sg_w, _, qm_w, ba_w = widths
    offs = ((0, 0, qkv_w), (0, qkv_w, qkv_w + z_w))
    lo = 0
    for w_ in widths[2:]:
        offs += ((1, lo, lo + w_),)
        lo += w_
    assert lo == w_b.shape[1]
    row = lambda w_: pl.BlockSpec((tm, w_), lambda i: (i, 0))
    out_shape = [jax.ShapeDtypeStruct((n, qkv_w), BF16 if conv else F32),
                 jax.ShapeDtypeStruct((n, z_w), BF16),
                 jax.ShapeDtypeStruct((n, ba_w), F32),
                 jax.ShapeDtypeStruct((n, sg_w), BF16),
                 jax.ShapeDtypeStruct((n, qm_w), BF16 if conv else F32)]
    out_specs = [row(qkv_w), row(z_w), row(ba_w), row(sg_w), row(qm_w)]
    in_specs = [row(d), _const_spec((1, d)), _const_spec((d, qkv_w + z_w)), _const_spec(w_b.shape),
                _const_spec((1, sg_w)), _const_spec(sgu_w.shape), _const_spec(sgu_bcols.shape)]
    args = [x2d, norm_mix.reshape(1, d), w_all, w_b, sgu_norm.reshape(1, -1), sgu_w, sgu_bcols]
    scratch = []
    seq_tiles = 1
    if conv:
        seq_tiles = seq_len // tm
        in_specs.append(_const_spec(conv_w.shape))
        args.append(conv_w)
        out_shape.append(jax.ShapeDtypeStruct((n // seq_len, SUBLANES, qkv_w), F32))
        out_specs.append(pl.BlockSpec((1, SUBLANES, qkv_w), lambda i: (i // seq_tiles, 0, 0)))
        scratch.append(pltpu.VMEM((SUBLANES, qkv_w), F32))
    else:
        out_shape.append(jax.ShapeDtypeStruct((n, sg_w), F32))
        out_specs.append(row(sg_w))
    kern = functools.partial(_in_proj_kernel, offs=offs, heads=heads, dk=dk,
                             sgu_groups=sgu_w.shape[0], sgu_blk=sgu_blk, seq_tiles=seq_tiles, conv=conv)
    return pl.pallas_call(
        kern,
        grid=(n // tm,),
        in_specs=in_specs,
        out_specs=tuple(out_specs),
        out_shape=tuple(out_shape),
        scratch_shapes=scratch,
        compiler_params=_cparams(1),
        name="in_proj_prompt" if conv else "in_proj_sample",
    )(*args)


def _log2(v):
    r = int(math.log2(v))
    assert 1 << r == v
    return r


def _iotas(n):
    ri = lax.broadcasted_iota(jnp.int32, (n, n), 0)
    ci = lax.broadcasted_iota(jnp.int32, (n, n), 1)
    return ri, ci


def _same_block(ri, ci, size):
    s = _log2(size)
    return (ri >> s) == (ci >> s)


def _unit_lower_inverse(lmats, ri, ci, blk, base):
    eye = (ri == ci).astype(F32)
    if base < blk:
        same = _same_block(ri, ci, base)
        lbs = [jnp.where(same, lm, 0.0) for lm in lmats]
    else:
        lbs = list(lmats)
    ps = lbs
    ts = [eye - lb for lb in lbs]
    k = 2
    while k < base:
        ps = [_mm(p, p) for p in ps]
        ts = [t + _mm(t, p) for t, p in zip(ts, ps)]
        k *= 2
    s = base
    while s < blk:
        sh = _log2(s)
        off = _same_block(ri, ci, 2 * s) & (((ri >> sh) & 1) == 1) & (((ci >> sh) & 1) == 0)
        xs = [_mm(jnp.where(off, lm, 0.0), t) for lm, t in zip(lmats, ts)]
        ts = [t - _mm(t, x) for t, x in zip(ts, xs)]
        s *= 2
    return ts


def _chunk_cumsum(g, chunk):
    n = g.shape[0]
    row = lax.broadcasted_iota(jnp.int32, g.shape, 0) & (chunk - 1)
    s = 1
    while s < chunk:
        shifted = pltpu.roll(g, s, 0)
        g = g + jnp.where(row >= s, shifted, 0.0)
        s *= 2
    return g


def _delta_gates(ba, alog_row, dtb_row, chunk):
    beta = _sigmoid(ba)
    g = -jnp.exp(alog_row) * _softplus(ba + dtb_row)
    gc = _chunk_cumsum(g, chunk)
    return beta, gc, gc.T


def _heads_prep(qs, ks, vs, betas, gcs, gcrows, glasts, causal, strict, ri, ci, blk, base):
    dv = vs[0].shape[-1]
    egs = [jnp.exp(gc) for gc in gcs]
    kbs = [k * b for k, b in zip(ks, betas)]
    vbs = [v * b for v, b in zip(vs, betas)]
    decays = [jnp.exp(jnp.where(causal, gc - gr, -jnp.inf)) for gc, gr in zip(gcs, gcrows)]
    lmats = [jnp.where(strict, _mm_nt(kb, k) * dec, 0.0) for kb, k, dec in zip(kbs, ks, decays)]
    a_intras = [_mm_nt(q, k) * dec for q, k, dec in zip(qs, ks, decays)]
    tmats = _unit_lower_inverse(lmats, ri, ci, blk, base)
    uws = [_mm(t, jnp.concatenate([vb, kb * eg], axis=1))
           for t, vb, kb, eg in zip(tmats, vbs, kbs, egs)]
    us = [uw[:, :dv] for uw in uws]
    ws = [uw[:, dv:] for uw in uws]
    q_decs = [q * eg for q, eg in zip(qs, egs)]
    k_decs = [k * jnp.exp(gl - gc) for k, gl, gc in zip(ks, glasts, gcs)]
    return us, ws, a_intras, q_decs, k_decs


def _out_norm_gate(o, dn_gain, zgate):
    return (_rms(o, dn_gain) * zgate).astype(BF16)


def _delta_prompt_kernel(qkv_ref, z_ref, ba_ref, alog_ref, dtb_ref, dng_ref,
                         o_ref, s_ref, *, tb, heads, dk):
    @pl.when(pl.program_id(1) == 0)
    def _():
        s_ref[...] = jnp.zeros_like(s_ref)

    qkv = qkv_ref[0]
    beta, gc, gct = _delta_gates(ba_ref[0], alog_ref[...], dtb_ref[...], DN_CHUNK)
    nchunk = tb // DN_CHUNK
    glast = jnp.concatenate(
        [jnp.broadcast_to(gc[(c + 1) * DN_CHUNK - 1:(c + 1) * DN_CHUNK, :], (DN_CHUNK, gc.shape[1]))
         for c in range(nchunk)], axis=0)

    ri, ci = _iotas(tb)
    same = _same_block(ri, ci, DN_CHUNK)
    causal = same & (ri >= ci)
    strict = same & (ri > ci)
    kw = heads * dk
    z = z_ref[0]
    dng = dng_ref[...]

    hs = range(heads)
    gcols = [heads + h for h in hs]
    us, ws, a_intras, q_decs, k_decs = _heads_prep(
        [qkv[:, h * dk:(h + 1) * dk].astype(F32) for h in hs],
        [qkv[:, kw + h * dk:kw + (h + 1) * dk].astype(F32) for h in hs],
        [qkv[:, 2 * kw + h * dk:2 * kw + (h + 1) * dk].astype(F32) for h in hs],
        [beta[:, h:h + 1] for h in hs],
        [gc[:, g:g + 1] for g in gcols],
        [gct[g:g + 1, :] for g in gcols],
        [glast[:, g:g + 1] for g in gcols],
        causal, strict, ri, ci, DN_CHUNK, 16)
    states = [s_ref[0, h] for h in hs]
    o_inter = [[] for _ in hs]
    v_news = [[] for _ in hs]
    for c in range(nchunk):
        r0, r1 = c * DN_CHUNK, (c + 1) * DN_CHUNK
        for h in hs:
            s = states[h]
            wq = _mm(jnp.concatenate([ws[h][r0:r1], q_decs[h][r0:r1]], axis=0), s)
            v_new = us[h][r0:r1] - wq[:DN_CHUNK]
            v_news[h].append(v_new)
            o_inter[h].append(wq[DN_CHUNK:])
            dl = jnp.exp(glast[r0:r0 + 1, gcols[h]:gcols[h] + 1])
            states[h] = s * dl + _mm_tn(k_decs[h][r0:r1], v_new)
    for h in hs:
        s_ref[0, h] = states[h]
        o = jnp.concatenate(o_inter[h], axis=0) + _mm(a_intras[h], jnp.concatenate(v_news[h], axis=0))
        o_ref[0, :, h * dk:(h + 1) * dk] = _out_norm_gate(o, dng, z[:, h * dk:(h + 1) * dk])


def _delta_prompt(qkv, zg, ba, alog_row, dtb_row, dn_norm, heads, dk, tb):
    b, t, c3 = qkv.shape
    vw = zg.shape[-1]
    kern = functools.partial(_delta_prompt_kernel, tb=tb, heads=heads, dk=dk)
    return pl.pallas_call(
        kern,
        grid=(b, t // tb),
        in_specs=[pl.BlockSpec((1, tb, c3), lambda i, j: (i, j, 0)),
                  pl.BlockSpec((1, tb, vw), lambda i, j: (i, j, 0)),
                  pl.BlockSpec((1, tb, LANES), lambda i, j: (i, j, 0)),
                  _const_spec((1, LANES)), _const_spec((1, LANES)), _const_spec((1, dk))],
        out_specs=(pl.BlockSpec((1, tb, vw), lambda i, j: (i, j, 0)),
                   pl.BlockSpec((1, heads, dk, dk), lambda i, j: (i, 0, 0, 0))),
        out_shape=(jax.ShapeDtypeStruct((b, t, vw), BF16),
                   jax.ShapeDtypeStruct((b, heads, dk, dk), F32)),
        compiler_params=_cparams(2),
        name="delta_prompt",
    )(qkv, zg, ba, alog_row, dtb_row, dn_norm.reshape(1, dk))


def _delta_sample_kernel(xq_ref, xk_ref, xv_ref, cwq_ref, cwk_ref, cwv_ref, z_ref, ba_ref,
                         alog_ref, dtb_ref, dng_ref, s0_ref, o_ref, s_ref, vn_ref, qs_ref,
                         *, nseq, t, heads, hg, dk):
    hb = pl.program_id(1) * hg
    rows = nseq * t
    grp = 2 * t
    js = range(hg)

    def conv(x_ref, cw_ref):
        x = x_ref[...]
        cw = cw_ref[...]
        y = x * cw[CONV_W - 1:CONV_W, :]
        for i in range(CONV_W - 1):
            y = y + pltpu.roll(x, CONV_W - 1 - i, 0) * cw[i:i + 1, :]
        y = y.reshape(nseq, grp, y.shape[-1])[:, t:, :].reshape(rows, y.shape[-1])
        return _silu(y)

    q = _l2norm_heads(conv(xq_ref, cwq_ref), hg, dk, dk ** -0.5)
    k = _l2norm_heads(conv(xk_ref, cwk_ref), hg, dk, 1.0)
    v = conv(xv_ref, cwv_ref)

    beta, gc, gct = _delta_gates(ba_ref[...], alog_ref[...], dtb_ref[...], t)
    lane = lax.broadcasted_iota(jnp.int32, (rows, LANES), 1)
    sub = lax.broadcasted_iota(jnp.int32, (LANES, rows), 0)
    beta_h = [jnp.sum(jnp.where(lane == hb + j, beta, 0.0), axis=1, keepdims=True) for j in js]
    gc_h = [jnp.sum(jnp.where(lane == heads + hb + j, gc, 0.0), axis=1, keepdims=True) for j in js]
    gcrow_h = [jnp.sum(jnp.where(sub == heads + hb + j, gct, 0.0), axis=0, keepdims=True) for j in js]
    glast_h = [jnp.concatenate(
        [jnp.broadcast_to(g[(b + 1) * t - 1:(b + 1) * t, :], (t, 1)) for b in range(nseq)], axis=0)
        for g in gc_h]

    ri, ci = _iotas(rows)
    same = _same_block(ri, ci, t)
    causal = same & (ri >= ci)
    strict = same & (ri > ci)
    hsl = [slice(j * dk, (j + 1) * dk) for j in js]
    us, ws, a_intras, q_decs, k_decs = _heads_prep(
        [q[:, c] for c in hsl], [k[:, c] for c in hsl], [v[:, c] for c in hsl],
        beta_h, gc_h, gcrow_h, glast_h, causal, strict, ri, ci, t, t)

    for b in range(nseq):
        r0, r1 = b * t, (b + 1) * t
        for j in js:
            wq = _mm(jnp.concatenate([ws[j][r0:r1], q_decs[j][r0:r1]], axis=0), s0_ref[b, j])
            vn_ref[j, r0:r1, :] = us[j][r0:r1] - wq[:t]
            qs_ref[j, r0:r1, :] = wq[t:]
    v_news = [vn_ref[j] for j in js]
    z = z_ref[...]
    for j in js:
        o = qs_ref[j] + _mm(a_intras[j], v_news[j])
        o_ref[:, hsl[j]] = _out_norm_gate(o, dng_ref[...], z[:, hsl[j]])

    kts = [kd.T for kd in k_decs]
    col_seq = lax.broadcasted_iota(jnp.int32, kts[0].shape, 1) >> _log2(t)
    v_new_b = [vn.astype(BF16) for vn in v_news]
    for b in range(nseq):
        for j in js:
            dl = jnp.exp(glast_h[j][b * t:b * t + 1, :])
            upd = jnp.dot(jnp.where(col_seq == b, kts[j], 0.0).astype(BF16), v_new_b[j],
                          preferred_element_type=F32)
            s_ref[b, j] = s0_ref[b, j] * dl + upd


def _delta_sample(xc, z, ba, conv_w, alog_row, dtb_row, dn_norm, s0, heads, dk, t, nseq, hg):
    rows_all = z.shape[0]
    b = rows_all // t
    ngrp = heads // hg
    kern = functools.partial(_delta_sample_kernel, nseq=nseq, t=t, heads=heads, hg=hg, dk=dk)
    xblk = (nseq * 2 * t, hg * dk)
    cwblk = (CONV_W, hg * dk)
    return pl.pallas_call(
        kern,
        grid=(b // nseq, ngrp),
        in_specs=[pl.BlockSpec(xblk, lambda i, h: (i, h)),
                  pl.BlockSpec(xblk, lambda i, h: (i, ngrp + h)),
                  pl.BlockSpec(xblk, lambda i, h: (i, 2 * ngrp + h)),
                  pl.BlockSpec(cwblk, lambda i, h: (0, h)),
                  pl.BlockSpec(cwblk, lambda i, h: (0, ngrp + h)),
                  pl.BlockSpec(cwblk, lambda i, h: (0, 2 * ngrp + h)),
                  pl.BlockSpec((nseq * t, hg * dk), lambda i, h: (i, h)),
                  pl.BlockSpec((nseq * t, LANES), lambda i, h: (i, 0)),
                  _const_spec((1, LANES)), _const_spec((1, LANES)), _const_spec((1, dk)),
                  pl.BlockSpec((nseq, hg, dk, dk), lambda i, h: (i, h, 0, 0))],
        out_specs=(pl.BlockSpec((nseq * t, hg * dk), lambda i, h: (i, h)),
                   pl.BlockSpec((nseq, hg, dk, dk), lambda i, h: (i, h, 0, 0))),
        out_shape=(jax.ShapeDtypeStruct((rows_all, heads * dk), BF16),
                   jax.ShapeDtypeStruct(s0.shape, F32)),
        scratch_shapes=[pltpu.VMEM((hg, nseq * t, dk), F32), pltpu.VMEM((hg, nseq * t, dk), F32)],
        compiler_params=_cparams(2),
        name="delta_sample",
    )(xc, xc, xc, conv_w, conv_w, conv_w, z, ba, alog_row, dtb_row, dn_norm.reshape(1, dk), s0)


def _mem_kv_kernel(m_ref, g_ref, w_ref, k_ref, v_ref):
    mn = _rms(m_ref[...], g_ref[...]).astype(BF16)
    half = k_ref.shape[1]
    k_ref[...] = jnp.dot(mn, w_ref[:, :half], preferred_element_type=F32)
    v_ref[...] = jnp.dot(mn, w_ref[:, half:], preferred_element_type=F32)


def _mem_kv(mem2d, gain, w_kv, tm):
    n, d = mem2d.shape
    half = w_kv.shape[1] // 2
    return pl.pallas_call(
        _mem_kv_kernel,
        grid=(n // tm,),
        in_specs=[pl.BlockSpec((tm, d), lambda i: (i, 0)), _const_spec((1, d)),
                  _const_spec(w_kv.shape)],
        out_specs=(pl.BlockSpec((tm, half), lambda i: (i, 0)),) * 2,
        out_shape=(jax.ShapeDtypeStruct((n, half), F32),) * 2,
        compiler_params=_cparams(1),
        name="mem_kv",
    )(mem2d, gain.reshape(1, d), w_kv)


def _attend(q, k, v, scale):
    s = _mm_nt(q, k) * scale
    e = jnp.exp(s - jnp.max(s, axis=-1, keepdims=True))
    return _mm(e, v) / jnp.sum(e, axis=-1, keepdims=True)


def _mem_attn_prompt_kernel(q_ref, k_ref, v_ref, o_ref, *, heads, hd):
    scale = hd ** -0.5
    for h in range(heads):
        sl = slice(h * hd, (h + 1) * hd)
        o_ref[0, :, sl] = _attend(q_ref[0, :, sl], k_ref[0, :, sl], v_ref[0, :, sl], scale).astype(BF16)


def _mem_attn_prompt(q, mk, mv, heads, tq):
    b, t, w = q.shape
    m = mk.shape[1]
    kern = functools.partial(_mem_attn_prompt_kernel, heads=heads, hd=w // heads)
    return pl.pallas_call(
        kern,
        grid=(b, t // tq),
        in_specs=[pl.BlockSpec((1, tq, w), lambda i, j: (i, j, 0)),
                  pl.BlockSpec((1, m, w), lambda i, j: (i, 0, 0)),
                  pl.BlockSpec((1, m, w), lambda i, j: (i, 0, 0))],
        out_specs=pl.BlockSpec((1, tq, w), lambda i, j: (i, j, 0)),
        out_shape=jax.ShapeDtypeStruct((b, t, w), BF16),
        compiler_params=_cparams(2),
        name="mem_attn_prompt",
    )(q, mk, mv)


def _mem_attn_sample_kernel(q_ref, k_ref, v_ref, o_ref, *, nseq, rows, heads, hd):
    scale = hd ** -0.5
    mh = k_ref.shape[1]
    same_head = ((lax.broadcasted_iota(jnp.int32, (rows, mh), 0) & (heads - 1))
                 == (lax.broadcasted_iota(jnp.int32, (rows, mh), 1) & (heads - 1)))
    for b in range(nseq):
        rs = slice(b * rows, (b + 1) * rows)
        s = jnp.where(same_head, _mm_nt(q_ref[rs, :], k_ref[b]) * scale, -jnp.inf)
        e = jnp.exp(s - jnp.max(s, axis=-1, keepdims=True))
        o_ref[rs, :] = _mm(e, v_ref[b]) / jnp.sum(e, axis=-1, keepdims=True)


def _mem_attn_sample(q_rows, mk, mv, heads, rows, nseq):
    n, hd = q_rows.shape
    b, mh, _ = mk.shape
    assert heads & (heads - 1) == 0
    kern = functools.partial(_mem_attn_sample_kernel, nseq=nseq, rows=rows, heads=heads, hd=hd)
    return pl.pallas_call(
        kern,
        grid=(b // nseq,),
        in_specs=[pl.BlockSpec((nseq * rows, hd), lambda i: (i, 0)),
                  pl.BlockSpec((nseq, mh, hd), lambda i: (i, 0, 0)),
                  pl.BlockSpec((nseq, mh, hd), lambda i: (i, 0, 0))],
        out_specs=pl.BlockSpec((nseq * rows, hd), lambda i: (i, 0)),
        out_shape=jax.ShapeDtypeStruct((n, hd), F32),
        compiler_params=_cparams(1),
        name="mem_attn_sample",
    )(q_rows, mk, mv)


def _merge_kernel(x_ref, odn_ref, osg_ref, omem_ref, g_ref, wg_ref, wdn_ref, wsg_ref, wmem_ref,
                  wo_ref, o_ref):
    x = x_ref[...]
    d = x.shape[-1]
    xn = _rms(x, g_ref[...]).astype(BF16)
    merged = None
    for idx, (b_ref, w_ref) in enumerate(((odn_ref, wdn_ref), (osg_ref, wsg_ref), (omem_ref, wmem_ref))):
        gate = _sigmoid(jnp.dot(xn, wg_ref[:, idx * d:(idx + 1) * d], preferred_element_type=F32))
        term = gate * _mm(b_ref[...], w_ref[...])
        merged = term if merged is None else merged + term
    o_ref[...] = x + _mm(merged, wo_ref[...])


def _merge(x2d, odn, osg, omem, norm_mix, wg, wdn, wsg, wmem, wo, tm):
    n, d = x2d.shape
    row = lambda a: pl.BlockSpec((tm, a.shape[1]), lambda i: (i, 0))
    return pl.pallas_call(
        _merge_kernel,
        grid=(n // tm,),
        in_specs=[row(x2d), row(odn), row(osg), row(omem), _const_spec((1, d)),
                  _const_spec(wg.shape), _const_spec(wdn.shape), _const_spec(wsg.shape),
                  _const_spec(wmem.shape), _const_spec(wo.shape)],
        out_specs=row(x2d),
        out_shape=jax.ShapeDtypeStruct((n, d), F32),
        compiler_params=_cparams(1),
        name="merge",
    )(x2d, odn, osg, omem, norm_mix.reshape(1, d), wg, wdn, wsg, wmem, wo)


def _ffn_kernel(x_ref, gf_ref, wgu_ref, wd_ref, gl_ref, o_ref):
    x = x_ref[...]
    dff = wd_ref.shape[0]
    hn = _rms(x, gf_ref[...]).astype(BF16)
    hg = jnp.dot(hn, wgu_ref[:, :dff], preferred_element_type=F32)
    hu = jnp.dot(hn, wgu_ref[:, dff:], preferred_element_type=F32)
    x2 = x + _mm(_silu(hg) * hu, wd_ref[...])
    o_ref[...] = _rms(x2, gl_ref[...])


def _ffn(x2d, norm_ffn, wgu, wd, norm_final, tm):
    n, d = x2d.shape
    return pl.pallas_call(
        _ffn_kernel,
        grid=(n // tm,),
        in_specs=[pl.BlockSpec((tm, d), lambda i: (i, 0)), _const_spec((1, d)),
                  _const_spec(wgu.shape), _const_spec(wd.shape), _const_spec((1, d))],
        out_specs=pl.BlockSpec((tm, d), lambda i: (i, 0)),
        out_shape=jax.ShapeDtypeStruct((n, d), F32),
        compiler_params=_cparams(1),
        name="ffn",
    )(x2d, norm_ffn.reshape(1, d), wgu, wd, norm_final.reshape(1, d))


def kernel(x_prompt, x_sample, state_delta, state_conv, cache_mem_k, cache_mem_v, mem_prompt,
           norm_mix, w_in, conv_w, a_log, dt_bias, dn_norm, sgu_norm, sgu_w, sgu_b,
           w_br_dn, w_br_sgu, w_br_mem, w_o, mem_norm, w_mem_kv, norm_ffn, w_gate_up, w_down,
           norm_final):
    depth = w_in.shape[0]
    assert depth == 1, "single-layer stack only"
    bp, tp, d = x_prompt.shape
    bs, ts, _ = x_sample.shape
    heads, dk, dv = state_delta.shape[2:]
    assert dk == dv == LANES and 2 * heads <= LANES and ts == SUBLANES
    kw = heads * dk
    qkv_w = 2 * kw + heads * dv
    sgu_groups, sgu_chunk, _ = sgu_w.shape[1:]
    sgu_width = sgu_norm.shape[-1]
    mem_tokens, mem_heads, mem_hd = cache_mem_k.shape[2:]
    mem_w = mem_heads * mem_hd
    l = 0

    splits = (qkv_w, heads * dv, heads, heads, sgu_width, sgu_width, mem_w, 3 * d)
    o = [0]
    for s in splits:
        o.append(o[-1] + s)
    w = w_in[l].astype(BF16)
    ba_pad = jnp.zeros((d, LANES - 2 * heads), w.dtype)
    assert o[2] % LANES == 0
    w_b = jnp.concatenate([w[:, o[4]:o[7]], w[:, o[2]:o[4]], ba_pad], axis=1)
    widths = (qkv_w, heads * dv, sgu_width, sgu_width, mem_w, LANES)
    w_gate = w[:, o[7]:o[8]]

    lane_pad = lambda vec: jnp.zeros((1, LANES), F32).at[0, heads:2 * heads].set(vec)
    alog_row = lane_pad(a_log[l])
    dtb_row = lane_pad(dt_bias[l])

    wdn = w_br_dn[l].astype(BF16)
    wsg = w_br_sgu[l].astype(BF16)
    wmem = w_br_mem[l].astype(BF16)
    wo = w_o[l].astype(BF16)
    wgu = w_gate_up[l].astype(BF16)
    wd = w_down[l].astype(BF16)
    wkv = w_mem_kv[l].astype(BF16)

    tm = 512

    np_ = bp * tp
    xp2 = x_prompt.reshape(np_, d)
    qkv_p, zg_p, ba_p, osg_p, qm_p, tail_p = _in_proj(
        xp2, norm_mix[l], w, w_b, sgu_norm[l], sgu_w[l], sgu_b[l].T, SGU_CHUNK, widths, tm, heads, dk,
        conv_w=conv_w[l], seq_len=tp)
    odn_p, s_p = _delta_prompt(qkv_p.reshape(bp, tp, qkv_w), zg_p.reshape(bp, tp, -1),
                               ba_p.reshape(bp, tp, LANES), alog_row, dtb_row,
                               dn_norm[l], heads, dk, 256)
    conv_p = tail_p[:, SUBLANES - (CONV_W - 1):, :]
    mk_p, mv_p = _mem_kv(mem_prompt.reshape(bp * mem_tokens, d), mem_norm[l], wkv, tm)
    mk_p = mk_p.reshape(bp, mem_tokens, mem_w)
    mv_p = mv_p.reshape(bp, mem_tokens, mem_w)
    omem_p = _mem_attn_prompt(qm_p.reshape(bp, tp, mem_w), mk_p, mv_p, mem_heads, 512)
    x1_p = _merge(xp2, odn_p.reshape(np_, -1), osg_p, omem_p.reshape(np_, mem_w), norm_mix[l],
                  w_gate, wdn, wsg, wmem, wo, tm)
    y_p = _ffn(x1_p, norm_ffn[l], wgu, wd, norm_final, tm)

    ns = bs * ts
    xs2 = x_sample.reshape(ns, d)
    reps = SGU_CHUNK // ts
    w_tiles = jnp.tile(sgu_w[l][:, :ts, :ts], (1, reps, reps))
    b_cols = jnp.tile(sgu_b[l][:, :ts], (1, reps)).T
    qkv_s, zg_s, ba_s, osg_s, qm_s, v_s = _in_proj(
        xs2, norm_mix[l], w, w_b, sgu_norm[l], w_tiles, b_cols, ts, widths, tm, heads, dk)
    qkv_s3 = qkv_s.reshape(bs, ts, qkv_w)
    hist = jnp.concatenate(
        [jnp.zeros((bs, ts - (CONV_W - 1), qkv_w), F32), state_conv[l]], axis=1)
    xc_s = jnp.concatenate([hist, qkv_s3], axis=1).reshape(bs * 2 * ts, qkv_w)
    nseq = LANES // ts
    odn_s, s_s = _delta_sample(xc_s, zg_s, ba_s, conv_w[l], alog_row, dtb_row, dn_norm[l],
                               state_delta[l], heads, dk, ts, nseq, 4)
    conv_s = qkv_s3[:, ts - (CONV_W - 1):, :]
    omem_s = _mem_attn_sample(qm_s.reshape(ns * mem_heads, mem_hd),
                              cache_mem_k[l].reshape(bs, mem_tokens * mem_heads, mem_hd),
                              cache_mem_v[l].reshape(bs, mem_tokens * mem_heads, mem_hd),
                              mem_heads, ts * mem_heads, 8).reshape(ns, mem_w)
    x1_s = _merge(xs2, odn_s, osg_s, omem_s, norm_mix[l], w_gate, wdn, wsg, wmem, wo, tm)
    y_s = _ffn(x1_s, norm_ffn[l], wgu, wd, norm_final, tm)

    return (y_p.reshape(bp, tp, d), y_s.reshape(bs, ts, d),
            s_p[None], conv_p[None],
            mk_p.reshape(1, bp, mem_tokens, mem_heads, mem_hd),
            mv_p.reshape(1, bp, mem_tokens, mem_heads, mem_hd),
            s_s[None], conv_s[None], v_s.reshape(1, bs, ts, sgu_width))
```

```python
import functools
import math

import jax
import jax.numpy as jnp
from jax import lax
from jax.experimental import pallas as pl
from jax.experimental.pallas import tpu as pltpu

F32 = jnp.float32
BF16 = jnp.bfloat16

RMS_EPS = 1e-6
L2_EPS = 1e-6
CONV_W = 4
DN_CHUNK = 64
SGU_CHUNK = 128
CONV_COLS = 512
LANES = 128
SUBLANES = 8
VMEM_LIMIT = 56 * 1024 * 1024


def _cparams(n_axes):
    return pltpu.CompilerParams(
        dimension_semantics=("arbitrary",) * n_axes, vmem_limit_bytes=VMEM_LIMIT)


def _mm(a, b):
    return jnp.dot(a.astype(BF16), b.astype(BF16), preferred_element_type=F32)


def _mm_nt(a, b):
    return lax.dot_general(a.astype(BF16), b.astype(BF16), (((1,), (1,)), ((), ())),
                           preferred_element_type=F32)


def _mm_tn(a, b):
    return lax.dot_general(a.astype(BF16), b.astype(BF16), (((0,), (0,)), ((), ())),
                           preferred_element_type=F32)


def _rms(x, gain):
    return x * lax.rsqrt(jnp.mean(x * x, axis=-1, keepdims=True) + RMS_EPS) * gain


NEG_LOG2E = -1.4426950408889634


def _sigmoid(x):
    return 1.0 / (1.0 + jnp.exp2(x * NEG_LOG2E))


def _silu(x):
    return x * _sigmoid(x)


def _softplus(x):
    return jnp.maximum(x, 0.0) + jnp.log1p(jnp.exp(-jnp.abs(x)))


def _const_spec(shape):
    nd = len(shape)
    return pl.BlockSpec(shape, lambda *_: (0,) * nd, pipeline_mode=pl.Buffered(1))


def _l2norm_heads(a, heads, dk, scale):
    cols = []
    for h in range(heads):
        ah = a[:, h * dk:(h + 1) * dk]
        cols.append(ah * (lax.rsqrt(jnp.sum(ah * ah, axis=-1, keepdims=True) + L2_EPS) * scale))
    return jnp.concatenate(cols, axis=1)


def _in_proj_kernel(*refs, offs, heads, dk, sgu_groups, sgu_blk, seq_tiles, conv):
    x_ref, g_ref, wa_ref, wb_ref, sgn_ref, sw_ref, sb_ref = refs[:7]
    if conv:
        cw_ref, qkv_ref, zg_ref, ba_ref, osg_ref, qm_ref, tail_ref, xc_ref = refs[7:]
    else:
        qkv_ref, zg_ref, ba_ref, osg_ref, qm_ref, vsg_ref = refs[7:]
    tm = x_ref.shape[0]
    xn = _rms(x_ref[...], g_ref[...]).astype(BF16)

    def proj(k):
        w_ref, lo, hi = (wa_ref, wb_ref)[offs[k][0]], offs[k][1], offs[k][2]
        return jnp.dot(xn, w_ref[:, lo:hi], preferred_element_type=F32)

    if conv:
        hd = SUBLANES

        @pl.when(pl.program_id(0) % seq_tiles == 0)
        def _():
            xc_ref[...] = jnp.zeros(xc_ref.shape, F32)

        cw = cw_ref[...]
        for c0 in range(0, offs[0][2], CONV_COLS):
            cs = slice(c0, c0 + CONV_COLS)
            raw = jnp.dot(xn, wa_ref[:, cs], preferred_element_type=F32)
            ext = jnp.concatenate([xc_ref[:, cs], raw], axis=0)
            y = raw * cw[CONV_W - 1:CONV_W, cs]
            for i in range(CONV_W - 1):
                y = y + pltpu.roll(ext, CONV_W - 1 - i, 0)[hd:] * cw[i:i + 1, cs]
            last = raw[tm - hd:]
            tail_ref[0, :, cs] = last
            xc_ref[:, cs] = last
            act = _silu(y)
            if c0 < 2 * heads * dk:
                act = _l2norm_heads(act, CONV_COLS // dk, dk, dk ** -0.5 if c0 < heads * dk else 1.0)
            qkv_ref[:, cs] = act.astype(BF16)
    else:
        qkv_ref[...] = proj(0)
    u = jax.nn.gelu(proj(2))
    v = _rms(jax.nn.gelu(proj(3)), sgn_ref[...])
    if not conv:
        vsg_ref[...] = v
    rows = sw_ref.shape[1]
    ri, ci = _iotas(rows)
    causal = _same_block(ri, ci, sgu_blk) & (ri >= ci)
    bias = sb_ref[...]
    gcw = u.shape[1] // sgu_groups
    vb = v.astype(BF16)
    for g in range(sgu_groups):
        wc = jnp.where(causal, sw_ref[g], 0.0).astype(BF16)
        cs = slice(g * gcw, (g + 1) * gcw)
        for rb in range(tm // rows):
            rs = slice(rb * rows, (rb + 1) * rows)
            mixed = jnp.dot(wc, vb[rs, cs], preferred_element_type=F32) + bias[:, g:g + 1]
            osg_ref[rs, cs] = (u[rs, cs] * mixed).astype(BF16)

    zg_ref[...] = _silu(proj(1)).astype(BF16)
    qm_ref[...] = proj(4).astype(qm_ref.dtype)
    ba_ref[...] = proj(5)


def _in_proj(x2d, norm_mix, w_a, w_b, sgu_norm, sgu_w, sgu_bcols, sgu_blk, widths, tm, heads, dk,
             conv_w=None, seq_len=None):
    n, d = x2d.shape
    conv = conv_w is not None
    qkv_w, z_w, sg_w, _, qm_w, ba_w = widths
    offs = ((0, 0, qkv_w), (0, qkv_w, qkv_w + z_w))
    lo = 0
    for w_ in widths[2:]:
        offs += ((1, lo, lo + w_),)
        lo += w_
    assert lo == w_b.shape[1]
    row = lambda w_: pl.BlockSpec((tm, w_), lambda i: (i, 0))
    out_shape = [jax.ShapeDtypeStruct((n, qkv_w), BF16 if conv else F32),
                 jax.ShapeDtypeStruct((n, z_w), BF16),
                 jax.ShapeDtypeStruct((n, ba_w), F32),
                 jax.ShapeDtypeStruct((n, sg_w), BF16),
                 jax.ShapeDtypeStruct((n, qm_w), BF16 if conv else F32)]
    out_specs = [row(qkv_w), row(z_w), row(ba_w), row(sg_w), row(qm_w)]
    in_specs = [row(d), _const_spec((1, d)), _const_spec(w_a.shape), _const_spec(w_b.shape),
                _const_spec((1, sg_w)), _const_spec(sgu_w.shape), _const_spec(sgu_bcols.shape)]
    args = [x2d, norm_mix.reshape(1, d), w_a, w_b, sgu_norm.reshape(1, -1), sgu_w, sgu_bcols]
    scratch = []
    seq_tiles = 1
    if conv:
        seq_tiles = seq_len // tm
        in_specs.append(_const_spec(conv_w.shape))
        args.append(conv_w)
        out_shape.append(jax.ShapeDtypeStruct((n // seq_len, SUBLANES, qkv_w), F32))
        out_specs.append(pl.BlockSpec((1, SUBLANES, qkv_w), lambda i: (i // seq_tiles, 0, 0)))
        scratch.append(pltpu.VMEM((SUBLANES, qkv_w), F32))
    else:
        out_shape.append(jax.ShapeDtypeStruct((n, sg_w), F32))
        out_specs.append(row(sg_w))
    kern = functools.partial(_in_proj_kernel, offs=offs, heads=heads, dk=dk,
                             sgu_groups=sgu_w.shape[0], sgu_blk=sgu_blk, seq_tiles=seq_tiles, conv=conv)
    return pl.pallas_call(
        kern,
        grid=(n // tm,),
        in_specs=in_specs,
        out_specs=tuple(out_specs),
        out_shape=tuple(out_shape),
        scratch_shapes=scratch,
        compiler_params=_cparams(1),
        name="in_proj_prompt" if conv else "in_proj_sample",
    )(*args)


def _log2(v):
    r = int(math.log2(v))
    assert 1 << r == v
    return r


def _iotas(n):
    ri = lax.broadcasted_iota(jnp.int32, (n, n), 0)
    ci = lax.broadcasted_iota(jnp.int32, (n, n), 1)
    return ri, ci


def _same_block(ri, ci, size):
    s = _log2(size)
    return (ri >> s) == (ci >> s)


def _unit_lower_inverse(lmats, ri, ci, blk, base):
    eye = (ri == ci).astype(F32)
    if base < blk:
        same = _same_block(ri, ci, base)
        lbs = [jnp.where(same, lm, 0.0) for lm in lmats]
    else:
        lbs = list(lmats)
    ps = lbs
    ts = [eye - lb for lb in lbs]
    k = 2
    while k < base:
        ps = [_mm(p, p) for p in ps]
        ts = [t + _mm(t, p) for t, p in zip(ts, ps)]
        k *= 2
    s = base
    while s < blk:
        sh = _log2(s)
        off = _same_block(ri, ci, 2 * s) & (((ri >> sh) & 1) == 1) & (((ci >> sh) & 1) == 0)
        xs = [_mm(jnp.where(off, lm, 0.0), t) for lm, t in zip(lmats, ts)]
        ts = [t - _mm(t, x) for t, x in zip(ts, xs)]
        s *= 2
    return ts


def _chunk_cumsum(g, chunk):
    n = g.shape[0]
    row = lax.broadcasted_iota(jnp.int32, g.shape, 0) & (chunk - 1)
    s = 1
    while s < chunk:
        shifted = pltpu.roll(g, s, 0)
        g = g + jnp.where(row >= s, shifted, 0.0)
        s *= 2
    return g


def _delta_gates(ba, alog_row, dtb_row, chunk):
    beta = _sigmoid(ba)
    g = -jnp.exp(alog_row) * _softplus(ba + dtb_row)
    gc = _chunk_cumsum(g, chunk)
    return beta, gc, gc.T


def _heads_prep(qs, ks, vs, betas, gcs, gcrows, glasts, causal, strict, ri, ci, blk, base):
    dv = vs[0].shape[-1]
    egs = [jnp.exp(gc) for gc in gcs]
    kbs = [k * b for k, b in zip(ks, betas)]
    vbs = [v * b for v, b in zip(vs, betas)]
    decays = [jnp.exp(jnp.where(causal, gc - gr, -jnp.inf)) for gc, gr in zip(gcs, gcrows)]
    lmats = [jnp.where(strict, _mm_nt(kb, k) * dec, 0.0) for kb, k, dec in zip(kbs, ks, decays)]
    a_intras = [_mm_nt(q, k) * dec for q, k, dec in zip(qs, ks, decays)]
    tmats = _unit_lower_inverse(lmats, ri, ci, blk, base)
    uws = [_mm(t, jnp.concatenate([vb, kb * eg], axis=1))
           for t, vb, kb, eg in zip(tmats, vbs, kbs, egs)]
    us = [uw[:, :dv] for uw in uws]
    ws = [uw[:, dv:] for uw in uws]
    q_decs = [q * eg for q, eg in zip(qs, egs)]
    k_decs = [k * jnp.exp(gl - gc) for k, gl, gc in zip(ks, glasts, gcs)]
    return us, ws, a_intras, q_decs, k_decs


def _out_norm_gate(o, dn_gain, zgate):
    return (_rms(o, dn_gain) * zgate).astype(BF16)


def _delta_prompt_kernel(qkv_ref, z_ref, ba_ref, alog_ref, dtb_ref, dng_ref,
                         o_ref, s_ref, *, tb, heads, dk):
    @pl.when(pl.program_id(1) == 0)
    def _():
        s_ref[...] = jnp.zeros_like(s_ref)

    qkv = qkv_ref[0]
    beta, gc, gct = _delta_gates(ba_ref[0], alog_ref[...], dtb_ref[...], DN_CHUNK)
    nchunk = tb // DN_CHUNK
    glast = jnp.concatenate(
        [jnp.broadcast_to(gc[(c + 1) * DN_CHUNK - 1:(c + 1) * DN_CHUNK, :], (DN_CHUNK, gc.shape[1]))
         for c in range(nchunk)], axis=0)

    ri, ci = _iotas(tb)
    same = _same_block(ri, ci, DN_CHUNK)
    causal = same & (ri >= ci)
    strict = same & (ri > ci)
    kw = heads * dk
    z = z_ref[0]
    dng = dng_ref[...]

    hs = range(heads)
    gcols = [heads + h for h in hs]
    us, ws, a_intras, q_decs, k_decs = _heads_prep(
        [qkv[:, h * dk:(h + 1) * dk].astype(F32) for h in hs],
        [qkv[:, kw + h * dk:kw + (h + 1) * dk].astype(F32) for h in hs],
        [qkv[:, 2 * kw + h * dk:2 * kw + (h + 1) * dk].astype(F32) for h in hs],
        [beta[:, h:h + 1] for h in hs],
        [gc[:, g:g + 1] for g in gcols],
        [gct[g:g + 1, :] for g in gcols],
        [glast[:, g:g + 1] for g in gcols],
        causal, strict, ri, ci, DN_CHUNK, 16)
    states = [s_ref[0, h] for h in hs]
    o_inter = [[] for _ in hs]
    v_news = [[] for _ in hs]
    for c in range(nchunk):
        r0, r1 = c * DN_CHUNK, (c + 1) * DN_CHUNK
        for h in hs:
            s = states[h]
            wq = _mm(jnp.concatenate([ws[h][r0:r1], q_decs[h][r0:r1]], axis=0), s)
            v_new = us[h][r0:r1] - wq[:DN_CHUNK]
            v_news[h].append(v_new)
            o_inter[h].append(wq[DN_CHUNK:])
            dl = jnp.exp(glast[r0:r0 + 1, gcols[h]:gcols[h] + 1])
            states[h] = s * dl + _mm_tn(k_decs[h][r0:r1], v_new)
    for h in hs:
        s_ref[0, h] = states[h]
        o = jnp.concatenate(o_inter[h], axis=0) + _mm(a_intras[h], jnp.concatenate(v_news[h], axis=0))
        o_ref[0, :, h * dk:(h + 1) * dk] = _out_norm_gate(o, dng, z[:, h * dk:(h + 1) * dk])


def _delta_prompt(qkv, zg, ba, alog_row, dtb_row, dn_norm, heads, dk, tb):
    b, t, c3 = qkv.shape
    vw = zg.shape[-1]
    kern = functools.partial(_delta_prompt_kernel, tb=tb, heads=heads, dk=dk)
    return pl.pallas_call(
        kern,
        grid=(b, t // tb),
        in_specs=[pl.BlockSpec((1, tb, c3), lambda i, j: (i, j, 0)),
                  pl.BlockSpec((1, tb, vw), lambda i, j: (i, j, 0)),
                  pl.BlockSpec((1, tb, LANES), lambda i, j: (i, j, 0)),
                  _const_spec((1, LANES)), _const_spec((1, LANES)), _const_spec((1, dk))],
        out_specs=(pl.BlockSpec((1, tb, vw), lambda i, j: (i, j, 0)),
                   pl.BlockSpec((1, heads, dk, dk), lambda i, j: (i, 0, 0, 0))),
        out_shape=(jax.ShapeDtypeStruct((b, t, vw), BF16),
                   jax.ShapeDtypeStruct((b, heads, dk, dk), F32)),
        compiler_params=_cparams(2),
        name="delta_prompt",
    )(qkv, zg, ba, alog_row, dtb_row, dn_norm.reshape(1, dk))


def _delta_sample_kernel(hq_ref, hk_ref, hv_ref, xq_ref, xk_ref, xv_ref, cwq_ref, cwk_ref, cwv_ref,
                         z_ref, ba_ref, alog_ref, dtb_ref, dng_ref, s0_ref, o_ref, s_ref, vn_ref, qs_ref,
                         *, nseq, t, heads, hg, dk):
    hb = pl.program_id(1) * hg
    rows = nseq * t
    grp = 2 * t
    js = range(hg)

    def conv(h_ref, x_ref, cw_ref):
        cols = x_ref.shape[1]
        x = jnp.concatenate([h_ref[...].reshape(nseq, t, cols), x_ref[...].reshape(nseq, t, cols)],
                            axis=1).reshape(nseq * grp, cols)
        cw = cw_ref[...]
        y = x * cw[CONV_W - 1:CONV_W, :]
        for i in range(CONV_W - 1):
            y = y + pltpu.roll(x, CONV_W - 1 - i, 0) * cw[i:i + 1, :]
        y = y.reshape(nseq, grp, y.shape[-1])[:, t:, :].reshape(rows, y.shape[-1])
        return _silu(y)

    q = _l2norm_heads(conv(hq_ref, xq_ref, cwq_ref), hg, dk, dk ** -0.5)
    k = _l2norm_heads(conv(hk_ref, xk_ref, cwk_ref), hg, dk, 1.0)
    v = conv(hv_ref, xv_ref, cwv_ref)

    beta, gc, gct = _delta_gates(ba_ref[...], alog_ref[...], dtb_ref[...], t)
    lane = lax.broadcasted_iota(jnp.int32, (rows, LANES), 1)
    sub = lax.broadcasted_iota(jnp.int32, (LANES, rows), 0)
    beta_h = [jnp.sum(jnp.where(lane == hb + j, beta, 0.0), axis=1, keepdims=True) for j in js]
    gc_h = [jnp.sum(jnp.where(lane == heads + hb + j, gc, 0.0), axis=1, keepdims=True) for j in js]
    gcrow_h = [jnp.sum(jnp.where(sub == heads + hb + j, gct, 0.0), axis=0, keepdims=True) for j in js]
    glast_h = [jnp.concatenate(
        [jnp.broadcast_to(g[(b + 1) * t - 1:(b + 1) * t, :], (t, 1)) for b in range(nseq)], axis=0)
        for g in gc_h]

    ri, ci = _iotas(rows)
    same = _same_block(ri, ci, t)
    causal = same & (ri >= ci)
    strict = same & (ri > ci)
    hsl = [slice(j * dk, (j + 1) * dk) for j in js]
    us, ws, a_intras, q_decs, k_decs = _heads_prep(
        [q[:, c] for c in hsl], [k[:, c] for c in hsl], [v[:, c] for c in hsl],
        beta_h, gc_h, gcrow_h, glast_h, causal, strict, ri, ci, t, t)

    for b in range(nseq):
        r0, r1 = b * t, (b + 1) * t
        for j in js:
            wq = _mm(jnp.concatenate([ws[j][r0:r1], q_decs[j][r0:r1]], axis=0), s0_ref[b, j])
            vn_ref[j, r0:r1, :] = us[j][r0:r1] - wq[:t]
            qs_ref[j, r0:r1, :] = wq[t:]
    v_news = [vn_ref[j] for j in js]
    z = z_ref[...]
    for j in js:
        o = qs_ref[j] + _mm(a_intras[j], v_news[j])
        o_ref[:, hsl[j]] = _out_norm_gate(o, dng_ref[...], z[:, hsl[j]])

    kts = [kd.T for kd in k_decs]
    col_seq = lax.broadcasted_iota(jnp.int32, kts[0].shape, 1) >> _log2(t)
    v_new_b = [vn.astype(BF16) for vn in v_news]
    for b in range(nseq):
        for j in js:
            dl = jnp.exp(glast_h[j][b * t:b * t + 1, :])
            upd = jnp.dot(jnp.where(col_seq == b, kts[j], 0.0).astype(BF16), v_new_b[j],
                          preferred_element_type=F32)
            s_ref[b, j] = s0_ref[b, j] * dl + upd


def _delta_sample(hist, raw, z, ba, conv_w, alog_row, dtb_row, dn_norm, s0, heads, dk, t, nseq, hg):
    rows_all = z.shape[0]
    b = rows_all // t
    ngrp = heads // hg
    kern = functools.partial(_delta_sample_kernel, nseq=nseq, t=t, heads=heads, hg=hg, dk=dk)
    xblk = (nseq * t, hg * dk)
    cwblk = (CONV_W, hg * dk)
    xspecs = [pl.BlockSpec(xblk, lambda i, h: (i, h)),
              pl.BlockSpec(xblk, lambda i, h: (i, ngrp + h)),
              pl.BlockSpec(xblk, lambda i, h: (i, 2 * ngrp + h))]
    return pl.pallas_call(
        kern,
        grid=(b // nseq, ngrp),
        in_specs=xspecs + xspecs + [
                  pl.BlockSpec(cwblk, lambda i, h: (0, h)),
                  pl.BlockSpec(cwblk, lambda i, h: (0, ngrp + h)),
                  pl.BlockSpec(cwblk, lambda i, h: (0, 2 * ngrp + h)),
                  pl.BlockSpec((nseq * t, hg * dk), lambda i, h: (i, h)),
                  pl.BlockSpec((nseq * t, LANES), lambda i, h: (i, 0)),
                  _const_spec((1, LANES)), _const_spec((1, LANES)), _const_spec((1, dk)),
                  pl.BlockSpec((nseq, hg, dk, dk), lambda i, h: (i, h, 0, 0))],
        out_specs=(pl.BlockSpec((nseq * t, hg * dk), lambda i, h: (i, h)),
                   pl.BlockSpec((nseq, hg, dk, dk), lambda i, h: (i, h, 0, 0))),
        out_shape=(jax.ShapeDtypeStruct((rows_all, heads * dk), BF16),
                   jax.ShapeDtypeStruct(s0.shape, F32)),
        scratch_shapes=[pltpu.VMEM((hg, nseq * t, dk), F32), pltpu.VMEM((hg, nseq * t, dk), F32)],
        compiler_params=_cparams(2),
        name="delta_sample",
    )(hist, hist, hist, raw, raw, raw, conv_w, conv_w, conv_w, z, ba, alog_row, dtb_row,
      dn_norm.reshape(1, dk), s0)


def _mem_kv_kernel(m_ref, g_ref, w_ref, k_ref, v_ref):
    mn = _rms(m_ref[...], g_ref[...]).astype(BF16)
    half = k_ref.shape[1]
    k_ref[...] = jnp.dot(mn, w_ref[:, :half], preferred_element_type=F32)
    v_ref[...] = jnp.dot(mn, w_ref[:, half:], preferred_element_type=F32)


def _mem_kv(mem2d, gain, w_kv, tm):
    n, d = mem2d.shape
    half = w_kv.shape[1] // 2
    return pl.pallas_call(
        _mem_kv_kernel,
        grid=(n // tm,),
        in_specs=[pl.BlockSpec((tm, d), lambda i: (i, 0)), _const_spec((1, d)),
                  _const_spec(w_kv.shape)],
        out_specs=(pl.BlockSpec((tm, half), lambda i: (i, 0)),) * 2,
        out_shape=(jax.ShapeDtypeStruct((n, half), F32),) * 2,
        compiler_params=_cparams(1),
        name="mem_kv",
    )(mem2d, gain.reshape(1, d), w_kv)


def _attend(q, k, v, scale):
    s = _mm_nt(q, k) * scale
    e = jnp.exp(s - jnp.max(s, axis=-1, keepdims=True))
    return _mm(e, v) / jnp.sum(e, axis=-1, keepdims=True)


def _mem_attn_sample_kernel(q_ref, k_ref, v_ref, o_ref, *, nseq, rows, heads, hd):
    scale = hd ** -0.5
    mh = k_ref.shape[1]
    same_head = ((lax.broadcasted_iota(jnp.int32, (rows, mh), 0) & (heads - 1))
                 == (lax.broadcasted_iota(jnp.int32, (rows, mh), 1) & (heads - 1)))
    for b in range(nseq):
        rs = slice(b * rows, (b + 1) * rows)
        s = jnp.where(same_head, _mm_nt(q_ref[rs, :], k_ref[b]) * scale, -jnp.inf)
        e = jnp.exp(s - jnp.max(s, axis=-1, keepdims=True))
        o_ref[rs, :] = _mm(e, v_ref[b]) / jnp.sum(e, axis=-1, keepdims=True)


def _mem_attn_sample(q_rows, mk, mv, heads, rows, nseq):
    n, hd = q_rows.shape
    b, mh, _ = mk.shape
    assert heads & (heads - 1) == 0
    kern = functools.partial(_mem_attn_sample_kernel, nseq=nseq, rows=rows, heads=heads, hd=hd)
    return pl.pallas_call(
        kern,
        grid=(b // nseq,),
        in_specs=[pl.BlockSpec((nseq * rows, hd), lambda i: (i, 0)),
                  pl.BlockSpec((nseq, mh, hd), lambda i: (i, 0, 0)),
                  pl.BlockSpec((nseq, mh, hd), lambda i: (i, 0, 0))],
        out_specs=pl.BlockSpec((nseq * rows, hd), lambda i: (i, 0)),
        out_shape=jax.ShapeDtypeStruct((n, hd), F32),
        compiler_params=_cparams(1),
        name="mem_attn_sample",
    )(q_rows, mk, mv)


def _merge_kernel(*refs, mem_heads):
    if mem_heads:
        (x_ref, odn_ref, osg_ref, qm_ref, mk_ref, mv_ref, g_ref, wg_ref, wdn_ref, wsg_ref, wmem_ref,
         wo_ref, o_ref) = refs
        hd = qm_ref.shape[1] // mem_heads
        omem = jnp.concatenate(
            [_attend(qm_ref[:, h * hd:(h + 1) * hd], mk_ref[0, :, h * hd:(h + 1) * hd],
                     mv_ref[0, :, h * hd:(h + 1) * hd], hd ** -0.5) for h in range(mem_heads)], axis=1)
    else:
        (x_ref, odn_ref, osg_ref, omem_ref, g_ref, wg_ref, wdn_ref, wsg_ref, wmem_ref,
         wo_ref, o_ref) = refs
        omem = omem_ref[...]
    x = x_ref[...]
    d = x.shape[-1]
    xn = _rms(x, g_ref[...]).astype(BF16)
    merged = None
    for idx, (br, w_ref) in enumerate(((odn_ref[...], wdn_ref), (osg_ref[...], wsg_ref), (omem, wmem_ref))):
        gate = _sigmoid(jnp.dot(xn, wg_ref[:, idx * d:(idx + 1) * d], preferred_element_type=F32))
        term = gate * _mm(br, w_ref[...])
        merged = term if merged is None else merged + term
    o_ref[...] = x + _mm(merged, wo_ref[...])


def _merge(x2d, odn, osg, mem, norm_mix, wg, wdn, wsg, wmem, wo, tm, mem_heads=0, seq_len=None):
    n, d = x2d.shape
    row = lambda a: pl.BlockSpec((tm, a.shape[1]), lambda i: (i, 0))
    if mem_heads:
        qm, mk, mv = mem
        seq_tiles = seq_len // tm
        kvspec = pl.BlockSpec((1,) + mk.shape[1:], lambda i: (i // seq_tiles, 0, 0))
        mem_specs, mem_args = [row(qm), kvspec, kvspec], [qm, mk, mv]
    else:
        mem_specs, mem_args = [row(mem)], [mem]
    return pl.pallas_call(
        functools.partial(_merge_kernel, mem_heads=mem_heads),
        grid=(n // tm,),
        in_specs=[row(x2d), row(odn), row(osg)] + mem_specs + [
                  _const_spec((1, d)),
                  _const_spec(wg.shape), _const_spec(wdn.shape), _const_spec(wsg.shape),
                  _const_spec(wmem.shape), _const_spec(wo.shape)],
        out_specs=row(x2d),
        out_shape=jax.ShapeDtypeStruct((n, d), F32),
        compiler_params=_cparams(1),
        name="merge_attn" if mem_heads else "merge",
    )(x2d, odn, osg, *mem_args, norm_mix.reshape(1, d), wg, wdn, wsg, wmem, wo)


def _ffn_kernel(x_ref, gf_ref, wgu_ref, wd_ref, gl_ref, o_ref):
    x = x_ref[...]
    dff = wd_ref.shape[0]
    hn = _rms(x, gf_ref[...]).astype(BF16)
    hg = jnp.dot(hn, wgu_ref[:, :dff], preferred_element_type=F32)
    hu = jnp.dot(hn, wgu_ref[:, dff:], preferred_element_type=F32)
    x2 = x + _mm(_silu(hg) * hu, wd_ref[...])
    o_ref[...] = _rms(x2, gl_ref[...])


def _ffn(x2d, norm_ffn, wgu, wd, norm_final, tm):
    n, d = x2d.shape
    return pl.pallas_call(
        _ffn_kernel,
        grid=(n // tm,),
        in_specs=[pl.BlockSpec((tm, d), lambda i: (i, 0)), _const_spec((1, d)),
                  _const_spec(wgu.shape), _const_spec(wd.shape), _const_spec((1, d))],
        out_specs=pl.BlockSpec((tm, d), lambda i: (i, 0)),
        out_shape=jax.ShapeDtypeStruct((n, d), F32),
        compiler_params=_cparams(1),
        name="ffn",
    )(x2d, norm_ffn.reshape(1, d), wgu, wd, norm_final.reshape(1, d))


def _split_w_in_kernel(w_ref, a_ref, b_ref, g_ref, *, o, pad):
    w = w_ref[...]
    a_ref[...] = w[:, o[0]:o[2]].astype(BF16)
    b_ref[...] = jnp.concatenate(
        [w[:, o[4]:o[7]], w[:, o[2]:o[4]], jnp.zeros((w.shape[0], pad), F32)], axis=1).astype(BF16)
    g_ref[...] = w[:, o[7]:o[8]].astype(BF16)


def _split_w_in(w, o, pad):
    d, cols = w.shape
    rows = LANES
    widths = (o[2] - o[0], o[7] - o[4] + o[4] - o[2] + pad, o[8] - o[7])
    return pl.pallas_call(
        functools.partial(_split_w_in_kernel, o=o, pad=pad),
        grid=(d // rows,),
        in_specs=[pl.BlockSpec((rows, cols), lambda i: (i, 0))],
        out_specs=tuple(pl.BlockSpec((rows, w_), lambda i: (i, 0)) for w_ in widths),
        out_shape=tuple(jax.ShapeDtypeStruct((d, w_), BF16) for w_ in widths),
        compiler_params=_cparams(1),
        name="split_w_in",
    )(w)


def kernel(x_prompt, x_sample, state_delta, state_conv, cache_mem_k, cache_mem_v, mem_prompt,
           norm_mix, w_in, conv_w, a_log, dt_bias, dn_norm, sgu_norm, sgu_w, sgu_b,
           w_br_dn, w_br_sgu, w_br_mem, w_o, mem_norm, w_mem_kv, norm_ffn, w_gate_up, w_down,
           norm_final):
    depth = w_in.shape[0]
    assert depth == 1, "single-layer stack only"
    bp, tp, d = x_prompt.shape
    bs, ts, _ = x_sample.shape
    heads, dk, dv = state_delta.shape[2:]
    assert dk == dv == LANES and 2 * heads <= LANES and ts == SUBLANES
    kw = heads * dk
    qkv_w = 2 * kw + heads * dv
    sgu_groups, sgu_chunk, _ = sgu_w.shape[1:]
    sgu_width = sgu_norm.shape[-1]
    mem_tokens, mem_heads, mem_hd = cache_mem_k.shape[2:]
    mem_w = mem_heads * mem_hd
    l = 0

    splits = (qkv_w, heads * dv, heads, heads, sgu_width, sgu_width, mem_w, 3 * d)
    o = [0]
    for s in splits:
        o.append(o[-1] + s)
    w_a, w_b, w_gate = _split_w_in(w_in[l], tuple(o), LANES - 2 * heads)
    widths = (qkv_w, heads * dv, sgu_width, sgu_width, mem_w, LANES)

    lane_pad = lambda vec: jnp.zeros((1, LANES), F32).at[0, heads:2 * heads].set(vec)
    alog_row = lane_pad(a_log[l])
    dtb_row = lane_pad(dt_bias[l])

    wdn = w_br_dn[l].astype(BF16)
    wsg = w_br_sgu[l].astype(BF16)
    wmem = w_br_mem[l].astype(BF16)
    wo = w_o[l].astype(BF16)
    wgu = w_gate_up[l].astype(BF16)
    wd = w_down[l].astype(BF16)
    wkv = w_mem_kv[l].astype(BF16)

    tm = 512

    np_ = bp * tp
    xp2 = x_prompt.reshape(np_, d)
    qkv_p, zg_p, ba_p, osg_p, qm_p, tail_p = _in_proj(
        xp2, norm_mix[l], w_a, w_b, sgu_norm[l], sgu_w[l], sgu_b[l].T, SGU_CHUNK, widths, tm, heads, dk,
        conv_w=conv_w[l], seq_len=tp)
    odn_p, s_p = _delta_prompt(qkv_p.reshape(bp, tp, qkv_w), zg_p.reshape(bp, tp, -1),
                               ba_p.reshape(bp, tp, LANES), alog_row, dtb_row,
                               dn_norm[l], heads, dk, 256)
    conv_p = tail_p[:, SUBLANES - (CONV_W - 1):, :]
    mk_p, mv_p = _mem_kv(mem_prompt.reshape(bp * mem_tokens, d), mem_norm[l], wkv, tm)
    mk_p = mk_p.reshape(bp, mem_tokens, mem_w)
    mv_p = mv_p.reshape(bp, mem_tokens, mem_w)
    x1_p = _merge(xp2, odn_p.reshape(np_, -1), osg_p, (qm_p, mk_p, mv_p), norm_mix[l],
                  w_gate, wdn, wsg, wmem, wo, tm, mem_heads=mem_heads, seq_len=tp)
    y_p = _ffn(x1_p, norm_ffn[l], wgu, wd, norm_final, tm)

    ns = bs * ts
    xs2 = x_sample.reshape(ns, d)
    reps = SGU_CHUNK // ts
    w_tiles = jnp.tile(sgu_w[l][:, :ts, :ts], (1, reps, reps))
    b_cols = jnp.tile(sgu_b[l][:, :ts], (1, reps)).T
    qkv_s, zg_s, ba_s, osg_s, qm_s, v_s = _in_proj(
        xs2, norm_mix[l], w_a, w_b, sgu_norm[l], w_tiles, b_cols, ts, widths, tm, heads, dk)
    qkv_s3 = qkv_s.reshape(bs, ts, qkv_w)
    hist = jnp.concatenate(
        [jnp.zeros((bs, ts - (CONV_W - 1), qkv_w), F32), state_conv[l]], axis=1).reshape(ns, qkv_w)
    nseq = LANES // ts
    odn_s, s_s = _delta_sample(hist, qkv_s, zg_s, ba_s, conv_w[l], alog_row, dtb_row, dn_norm[l],
                               state_delta[l], heads, dk, ts, nseq, 4)
    conv_s = qkv_s3[:, ts - (CONV_W - 1):, :]
    omem_s = _mem_attn_sample(qm_s.reshape(ns * mem_heads, mem_hd),
                              cache_mem_k[l].reshape(bs, mem_tokens * mem_heads, mem_hd),
                              cache_mem_v[l].reshape(bs, mem_tokens * mem_heads, mem_hd),
                              mem_heads, ts * mem_heads, 8).reshape(ns, mem_w)
    x1_s = _merge(xs2, odn_s, osg_s, omem_s, norm_mix[l], w_gate, wdn, wsg, wmem, wo, tm)
    y_s = _ffn(x1_s, norm_ffn[l], wgu, wd, norm_final, tm)

    return (y_p.reshape(bp, tp, d), y_s.reshape(bs, ts, d),
            s_p[None], conv_p[None],
            mk_p.reshape(1, bp, mem_tokens, mem_heads, mem_hd),
            mv_p.reshape(1, bp, mem_tokens, mem_heads, mem_hd),
            s_s[None], conv_s[None], v_s.reshape(1, bs, ts, sgu_width))
```

```python
import functools
import math

import jax
import jax.numpy as jnp
from jax import lax
from jax.experimental import pallas as pl
from jax.experimental.pallas import tpu as pltpu

F32 = jnp.float32
BF16 = jnp.bfloat16

RMS_EPS = 1e-6
L2_EPS = 1e-6
CONV_W = 4
DN_CHUNK = 64
SGU_CHUNK = 128
CONV_COLS = 512
LANES = 128
SUBLANES = 8
VMEM_LIMIT = 56 * 1024 * 1024


def _cparams(n_axes):
    return pltpu.CompilerParams(
        dimension_semantics=("arbitrary",) * n_axes, vmem_limit_bytes=VMEM_LIMIT)


def _mm(a, b):
    return jnp.dot(a.astype(BF16), b.astype(BF16), preferred_element_type=F32)


def _mm_nt(a, b):
    return lax.dot_general(a.astype(BF16), b.astype(BF16), (((1,), (1,)), ((), ())),
                           preferred_element_type=F32)


def _mm_tn(a, b):
    return lax.dot_general(a.astype(BF16), b.astype(BF16), (((0,), (0,)), ((), ())),
                           preferred_element_type=F32)


def _rms(x, gain):
    return x * lax.rsqrt(jnp.mean(x * x, axis=-1, keepdims=True) + RMS_EPS) * gain


NEG_LOG2E = -1.4426950408889634


def _sigmoid(x):
    return 1.0 / (1.0 + jnp.exp2(x * NEG_LOG2E))


def _silu(x):
    return x * _sigmoid(x)


def _softplus(x):
    return jnp.maximum(x, 0.0) + jnp.log1p(jnp.exp(-jnp.abs(x)))


def _row_window_spec(start, rows, cols):
    return pl.BlockSpec((pl.Element(rows), pl.Element(cols)), lambda *_: (start, 0),
                        pipeline_mode=pl.Buffered(1))


def _const_spec(shape):
    nd = len(shape)
    return pl.BlockSpec(shape, lambda *_: (0,) * nd, pipeline_mode=pl.Buffered(1))


def _l2norm_heads(a, heads, dk, scale):
    cols = []
    for h in range(heads):
        ah = a[:, h * dk:(h + 1) * dk]
        cols.append(ah * (lax.rsqrt(jnp.sum(ah * ah, axis=-1, keepdims=True) + L2_EPS) * scale))
    return jnp.concatenate(cols, axis=1)


def _in_proj_kernel(*refs, offs, heads, dk, sgu_groups, sgu_blk, seq_tiles, conv):
    x_ref, g_ref, wa_ref, wb_ref, sgn_ref, sw_ref, sb_ref = refs[:7]
    if conv:
        cw_ref, qkv_ref, zg_ref, ba_ref, osg_ref, qm_ref, tail_ref, xc_ref = refs[7:]
    else:
        qkv_ref, zg_ref, ba_ref, osg_ref, qm_ref, vsg_ref = refs[7:]
    tm = x_ref.shape[0]
    xn = _rms(x_ref[...], g_ref[...]).astype(BF16)

    def proj(k):
        w_ref, lo, hi = (wa_ref, wb_ref)[offs[k][0]], offs[k][1], offs[k][2]
        return _mm_nt(xn, w_ref[lo:hi, :])

    if conv:
        hd = SUBLANES

        @pl.when(pl.program_id(0) % seq_tiles == 0)
        def _():
            xc_ref[...] = jnp.zeros(xc_ref.shape, F32)

        cw = cw_ref[...]
        for c0 in range(0, offs[0][2], CONV_COLS):
            cs = slice(c0, c0 + CONV_COLS)
            raw = _mm_nt(xn, wa_ref[cs, :])
            ext = jnp.concatenate([xc_ref[:, cs], raw], axis=0)
            y = raw * cw[CONV_W - 1:CONV_W, cs]
            for i in range(CONV_W - 1):
                y = y + pltpu.roll(ext, CONV_W - 1 - i, 0)[hd:] * cw[i:i + 1, cs]
            last = raw[tm - hd:]
            tail_ref[0, :, cs] = last
            xc_ref[:, cs] = last
            act = _silu(y)
            if c0 < 2 * heads * dk:
                act = _l2norm_heads(act, CONV_COLS // dk, dk, dk ** -0.5 if c0 < heads * dk else 1.0)
            qkv_ref[:, cs] = act.astype(BF16)
    else:
        qkv_ref[...] = proj(0)
    u = jax.nn.gelu(proj(2))
    v = _rms(jax.nn.gelu(proj(3)), sgn_ref[...])
    if not conv:
        vsg_ref[...] = v
    rows = sw_ref.shape[1]
    ri, ci = _iotas(rows)
    causal = _same_block(ri, ci, sgu_blk) & (ri >= ci)
    bias = sb_ref[...]
    gcw = u.shape[1] // sgu_groups
    vb = v.astype(BF16)
    for g in range(sgu_groups):
        wc = jnp.where(causal, sw_ref[g], 0.0).astype(BF16)
        cs = slice(g * gcw, (g + 1) * gcw)
        for rb in range(tm // rows):
            rs = slice(rb * rows, (rb + 1) * rows)
            mixed = jnp.dot(wc, vb[rs, cs], preferred_element_type=F32) + bias[:, g:g + 1]
            osg_ref[rs, cs] = (u[rs, cs] * mixed).astype(BF16)

    zg_ref[...] = _silu(proj(1)).astype(BF16)
    qm_ref[...] = proj(4).astype(qm_ref.dtype)
    ba_ref[...] = proj(5)


def _in_proj(x2d, norm_mix, w_t, sgu_norm, sgu_w, sgu_bcols, sgu_blk, widths, tm, heads, dk,
             conv_w=None, seq_len=None):
    n, d = x2d.shape
    conv = conv_w is not None
    qkv_w, z_w, sg_w, _, qm_w, ba_w = widths
    n_ba = 2 * heads
    a_rows = qkv_w + z_w
    b_rows = n_ba + 2 * sg_w + qm_w
    offs = ((0, 0, qkv_w), (0, qkv_w, a_rows),
            (1, n_ba, n_ba + sg_w), (1, n_ba + sg_w, n_ba + 2 * sg_w),
            (1, n_ba + 2 * sg_w, b_rows), (1, 0, ba_w))
    row = lambda w_: pl.BlockSpec((tm, w_), lambda i: (i, 0))
    out_shape = [jax.ShapeDtypeStruct((n, qkv_w), BF16 if conv else F32),
                 jax.ShapeDtypeStruct((n, z_w), BF16),
                 jax.ShapeDtypeStruct((n, ba_w), F32),
                 jax.ShapeDtypeStruct((n, sg_w), BF16),
                 jax.ShapeDtypeStruct((n, qm_w), BF16 if conv else F32)]
    out_specs = [row(qkv_w), row(z_w), row(ba_w), row(sg_w), row(qm_w)]
    in_specs = [row(d), _const_spec((1, d)),
                _row_window_spec(0, a_rows, d), _row_window_spec(a_rows, b_rows, d),
                _const_spec((1, sg_w)), _const_spec(sgu_w.shape), _const_spec(sgu_bcols.shape)]
    args = [x2d, norm_mix.reshape(1, d), w_t, w_t, sgu_norm.reshape(1, -1), sgu_w, sgu_bcols]
    scratch = []
    seq_tiles = 1
    if conv:
        seq_tiles = seq_len // tm
        in_specs.append(_const_spec(conv_w.shape))
        args.append(conv_w)
        out_shape.append(jax.ShapeDtypeStruct((n // seq_len, SUBLANES, qkv_w), F32))
        out_specs.append(pl.BlockSpec((1, SUBLANES, qkv_w), lambda i: (i // seq_tiles, 0, 0)))
        scratch.append(pltpu.VMEM((SUBLANES, qkv_w), F32))
    else:
        out_shape.append(jax.ShapeDtypeStruct((n, sg_w), F32))
        out_specs.append(row(sg_w))
    kern = functools.partial(_in_proj_kernel, offs=offs, heads=heads, dk=dk,
                             sgu_groups=sgu_w.shape[0], sgu_blk=sgu_blk, seq_tiles=seq_tiles, conv=conv)
    return pl.pallas_call(
        kern,
        grid=(n // tm,),
        in_specs=in_specs,
        out_specs=tuple(out_specs),
        out_shape=tuple(out_shape),
        scratch_shapes=scratch,
        compiler_params=_cparams(1),
        name="in_proj_prompt" if conv else "in_proj_sample",
    )(*args)


def _log2(v):
    r = int(math.log2(v))
    assert 1 << r == v
    return r


def _iotas(n):
    ri = lax.broadcasted_iota(jnp.int32, (n, n), 0)
    ci = lax.broadcasted_iota(jnp.int32, (n, n), 1)
    return ri, ci


def _same_block(ri, ci, size):
    s = _log2(size)
    return (ri >> s) == (ci >> s)


def _unit_lower_inverse(lmats, ri, ci, blk, base):
    eye = (ri == ci).astype(F32)
    if base < blk:
        same = _same_block(ri, ci, base)
        lbs = [jnp.where(same, lm, 0.0) for lm in lmats]
    else:
        lbs = list(lmats)
    ps = lbs
    ts = [eye - lb for lb in lbs]
    k = 2
    while k < base:
        ps = [_mm(p, p) for p in ps]
        ts = [t + _mm(t, p) for t, p in zip(ts, ps)]
        k *= 2
    s = base
    while s < blk:
        sh = _log2(s)
        off = _same_block(ri, ci, 2 * s) & (((ri >> sh) & 1) == 1) & (((ci >> sh) & 1) == 0)
        xs = [_mm(jnp.where(off, lm, 0.0), t) for lm, t in zip(lmats, ts)]
        ts = [t - _mm(t, x) for t, x in zip(ts, xs)]
        s *= 2
    return ts


def _chunk_cumsum(g, chunk):
    n = g.shape[0]
    row = lax.broadcasted_iota(jnp.int32, g.shape, 0) & (chunk - 1)
    s = 1
    while s < chunk:
        shifted = pltpu.roll(g, s, 0)
        g = g + jnp.where(row >= s, shifted, 0.0)
        s *= 2
    return g


def _delta_gates(ba, alog_row, dtb_row, chunk):
    beta = _sigmoid(ba)
    g = -jnp.exp(alog_row) * _softplus(ba + dtb_row)
    gc = _chunk_cumsum(g, chunk)
    return beta, gc, gc.T


def _heads_prep(qs, ks, vs, betas, gcs, gcrows, glasts, causal, strict, ri, ci, blk, base):
    dv = vs[0].shape[-1]
    egs = [jnp.exp(gc) for gc in gcs]
    kbs = [k * b for k, b in zip(ks, betas)]
    vbs = [v * b for v, b in zip(vs, betas)]
    decays = [jnp.exp(jnp.where(causal, gc - gr, -jnp.inf)) for gc, gr in zip(gcs, gcrows)]
    lmats = [jnp.where(strict, _mm_nt(kb, k) * dec, 0.0) for kb, k, dec in zip(kbs, ks, decays)]
    a_intras = [_mm_nt(q, k) * dec for q, k, dec in zip(qs, ks, decays)]
    tmats = _unit_lower_inverse(lmats, ri, ci, blk, base)
    uws = [_mm(t, jnp.concatenate([vb, kb * eg], axis=1))
           for t, vb, kb, eg in zip(tmats, vbs, kbs, egs)]
    us = [uw[:, :dv] for uw in uws]
    ws = [uw[:, dv:] for uw in uws]
    q_decs = [q * eg for q, eg in zip(qs, egs)]
    k_decs = [k * jnp.exp(gl - gc) for k, gl, gc in zip(ks, glasts, gcs)]
    return us, ws, a_intras, q_decs, k_decs


def _out_norm_gate(o, dn_gain, zgate):
    return (_rms(o, dn_gain) * zgate).astype(BF16)


def _delta_prompt_kernel(qkv_ref, z_ref, ba_ref, alog_ref, dtb_ref, dng_ref,
                         o_ref, s_ref, *, tb, heads, dk):
    @pl.when(pl.program_id(1) == 0)
    def _():
        s_ref[...] = jnp.zeros_like(s_ref)

    qkv = qkv_ref[0]
    beta, gc, gct = _delta_gates(ba_ref[0], alog_ref[...], dtb_ref[...], DN_CHUNK)
    nchunk = tb // DN_CHUNK
    glast = jnp.concatenate(
        [jnp.broadcast_to(gc[(c + 1) * DN_CHUNK - 1:(c + 1) * DN_CHUNK, :], (DN_CHUNK, gc.shape[1]))
         for c in range(nchunk)], axis=0)

    ri, ci = _iotas(tb)
    same = _same_block(ri, ci, DN_CHUNK)
    causal = same & (ri >= ci)
    strict = same & (ri > ci)
    kw = heads * dk
    z = z_ref[0]
    dng = dng_ref[...]

    hs = range(heads)
    gcols = [heads + h for h in hs]
    us, ws, a_intras, q_decs, k_decs = _heads_prep(
        [qkv[:, h * dk:(h + 1) * dk].astype(F32) for h in hs],
        [qkv[:, kw + h * dk:kw + (h + 1) * dk].astype(F32) for h in hs],
        [qkv[:, 2 * kw + h * dk:2 * kw + (h + 1) * dk].astype(F32) for h in hs],
        [beta[:, h:h + 1] for h in hs],
        [gc[:, g:g + 1] for g in gcols],
        [gct[g:g + 1, :] for g in gcols],
        [glast[:, g:g + 1] for g in gcols],
        causal, strict, ri, ci, DN_CHUNK, 16)
    states = [s_ref[0, h] for h in hs]
    o_inter = [[] for _ in hs]
    v_news = [[] for _ in hs]
    for c in range(nchunk):
        r0, r1 = c * DN_CHUNK, (c + 1) * DN_CHUNK
        for h in hs:
            s = states[h]
            wq = _mm(jnp.concatenate([ws[h][r0:r1], q_decs[h][r0:r1]], axis=0), s)
            v_new = us[h][r0:r1] - wq[:DN_CHUNK]
            v_news[h].append(v_new)
            o_inter[h].append(wq[DN_CHUNK:])
            dl = jnp.exp(glast[r0:r0 + 1, gcols[h]:gcols[h] + 1])
            states[h] = s * dl + _mm_tn(k_decs[h][r0:r1], v_new)
    for h in hs:
        s_ref[0, h] = states[h]
        o = jnp.concatenate(o_inter[h], axis=0) + _mm(a_intras[h], jnp.concatenate(v_news[h], axis=0))
        o_ref[0, :, h * dk:(h + 1) * dk] = _out_norm_gate(o, dng, z[:, h * dk:(h + 1) * dk])


def _delta_prompt(qkv, zg, ba, alog_row, dtb_row, dn_norm, heads, dk, tb):
    b, t, c3 = qkv.shape
    vw = zg.shape[-1]
    kern = functools.partial(_delta_prompt_kernel, tb=tb, heads=heads, dk=dk)
    return pl.pallas_call(
        kern,
        grid=(b, t // tb),
        in_specs=[pl.BlockSpec((1, tb, c3), lambda i, j: (i, j, 0)),
                  pl.BlockSpec((1, tb, vw), lambda i, j: (i, j, 0)),
                  pl.BlockSpec((1, tb, LANES), lambda i, j: (i, j, 0)),
                  _const_spec((1, LANES)), _const_spec((1, LANES)), _const_spec((1, dk))],
        out_specs=(pl.BlockSpec((1, tb, vw), lambda i, j: (i, j, 0)),
                   pl.BlockSpec((1, heads, dk, dk), lambda i, j: (i, 0, 0, 0))),
        out_shape=(jax.ShapeDtypeStruct((b, t, vw), BF16),
                   jax.ShapeDtypeStruct((b, heads, dk, dk), F32)),
        compiler_params=_cparams(2),
        name="delta_prompt",
    )(qkv, zg, ba, alog_row, dtb_row, dn_norm.reshape(1, dk))


def _delta_sample_kernel(hq_ref, hk_ref, hv_ref, xq_ref, xk_ref, xv_ref, cwq_ref, cwk_ref, cwv_ref,
                         z_ref, ba_ref, alog_ref, dtb_ref, dng_ref, s0_ref, o_ref, s_ref, vn_ref, qs_ref,
                         *, nseq, t, heads, hg, dk):
    hb = pl.program_id(1) * hg
    rows = nseq * t
    grp = 2 * t
    js = range(hg)

    def conv(h_ref, x_ref, cw_ref):
        cols = x_ref.shape[1]
        x = jnp.concatenate([h_ref[...].reshape(nseq, t, cols), x_ref[...].reshape(nseq, t, cols)],
                            axis=1).reshape(nseq * grp, cols)
        cw = cw_ref[...]
        y = x * cw[CONV_W - 1:CONV_W, :]
        for i in range(CONV_W - 1):
            y = y + pltpu.roll(x, CONV_W - 1 - i, 0) * cw[i:i + 1, :]
        y = y.reshape(nseq, grp, y.shape[-1])[:, t:, :].reshape(rows, y.shape[-1])
        return _silu(y)

    q = _l2norm_heads(conv(hq_ref, xq_ref, cwq_ref), hg, dk, dk ** -0.5)
    k = _l2norm_heads(conv(hk_ref, xk_ref, cwk_ref), hg, dk, 1.0)
    v = conv(hv_ref, xv_ref, cwv_ref)

    beta, gc, gct = _delta_gates(ba_ref[...], alog_ref[...], dtb_ref[...], t)
    lane = lax.broadcasted_iota(jnp.int32, (rows, LANES), 1)
    sub = lax.broadcasted_iota(jnp.int32, (LANES, rows), 0)
    beta_h = [jnp.sum(jnp.where(lane == hb + j, beta, 0.0), axis=1, keepdims=True) for j in js]
    gc_h = [jnp.sum(jnp.where(lane == heads + hb + j, gc, 0.0), axis=1, keepdims=True) for j in js]
    gcrow_h = [jnp.sum(jnp.where(sub == heads + hb + j, gct, 0.0), axis=0, keepdims=True) for j in js]
    glast_h = [jnp.concatenate(
        [jnp.broadcast_to(g[(b + 1) * t - 1:(b + 1) * t, :], (t, 1)) for b in range(nseq)], axis=0)
        for g in gc_h]

    ri, ci = _iotas(rows)
    same = _same_block(ri, ci, t)
    causal = same & (ri >= ci)
    strict = same & (ri > ci)
    hsl = [slice(j * dk, (j + 1) * dk) for j in js]
    us, ws, a_intras, q_decs, k_decs = _heads_prep(
        [q[:, c] for c in hsl], [k[:, c] for c in hsl], [v[:, c] for c in hsl],
        beta_h, gc_h, gcrow_h, glast_h, causal, strict, ri, ci, t, t)

    for b in range(nseq):
        r0, r1 = b * t, (b + 1) * t
        for j in js:
            wq = _mm(jnp.concatenate([ws[j][r0:r1], q_decs[j][r0:r1]], axis=0), s0_ref[b, j])
            vn_ref[j, r0:r1, :] = us[j][r0:r1] - wq[:t]
            qs_ref[j, r0:r1, :] = wq[t:]
    v_news = [vn_ref[j] for j in js]
    z = z_ref[...]
    for j in js:
        o = qs_ref[j] + _mm(a_intras[j], v_news[j])
        o_ref[:, hsl[j]] = _out_norm_gate(o, dng_ref[...], z[:, hsl[j]])

    kts = [kd.T for kd in k_decs]
    col_seq = lax.broadcasted_iota(jnp.int32, kts[0].shape, 1) >> _log2(t)
    v_new_b = [vn.astype(BF16) for vn in v_news]
    for b in range(nseq):
        for j in js:
            dl = jnp.exp(glast_h[j][b * t:b * t + 1, :])
            upd = jnp.dot(jnp.where(col_seq == b, kts[j], 0.0).astype(BF16), v_new_b[j],
                          preferred_element_type=F32)
            s_ref[b, j] = s0_ref[b, j] * dl + upd


def _delta_sample(hist, raw, z, ba, conv_w, alog_row, dtb_row, dn_norm, s0, heads, dk, t, nseq, hg):
    rows_all = z.shape[0]
    b = rows_all // t
    ngrp = heads // hg
    kern = functools.partial(_delta_sample_kernel, nseq=nseq, t=t, heads=heads, hg=hg, dk=dk)
    xblk = (nseq * t, hg * dk)
    cwblk = (CONV_W, hg * dk)
    xspecs = [pl.BlockSpec(xblk, lambda i, h: (i, h)),
              pl.BlockSpec(xblk, lambda i, h: (i, ngrp + h)),
              pl.BlockSpec(xblk, lambda i, h: (i, 2 * ngrp + h))]
    return pl.pallas_call(
        kern,
        grid=(b // nseq, ngrp),
        in_specs=xspecs + xspecs + [
                  pl.BlockSpec(cwblk, lambda i, h: (0, h)),
                  pl.BlockSpec(cwblk, lambda i, h: (0, ngrp + h)),
                  pl.BlockSpec(cwblk, lambda i, h: (0, 2 * ngrp + h)),
                  pl.BlockSpec((nseq * t, hg * dk), lambda i, h: (i, h)),
                  pl.BlockSpec((nseq * t, LANES), lambda i, h: (i, 0)),
                  _const_spec((1, LANES)), _const_spec((1, LANES)), _const_spec((1, dk)),
                  pl.BlockSpec((nseq, hg, dk, dk), lambda i, h: (i, h, 0, 0))],
        out_specs=(pl.BlockSpec((nseq * t, hg * dk), lambda i, h: (i, h)),
                   pl.BlockSpec((nseq, hg, dk, dk), lambda i, h: (i, h, 0, 0))),
        out_shape=(jax.ShapeDtypeStruct((rows_all, heads * dk), BF16),
                   jax.ShapeDtypeStruct(s0.shape, F32)),
        scratch_shapes=[pltpu.VMEM((hg, nseq * t, dk), F32), pltpu.VMEM((hg, nseq * t, dk), F32)],
        compiler_params=_cparams(2),
        name="delta_sample",
    )(hist, hist, hist, raw, raw, raw, conv_w, conv_w, conv_w, z, ba, alog_row, dtb_row,
      dn_norm.reshape(1, dk), s0)


def _mem_kv_kernel(m_ref, g_ref, w_ref, k_ref, v_ref):
    mn = _rms(m_ref[...], g_ref[...]).astype(BF16)
    half = k_ref.shape[1]
    k_ref[...] = jnp.dot(mn, w_ref[:, :half], preferred_element_type=F32)
    v_ref[...] = jnp.dot(mn, w_ref[:, half:], preferred_element_type=F32)


def _mem_kv(mem2d, gain, w_kv, tm):
    n, d = mem2d.shape
    half = w_kv.shape[1] // 2
    return pl.pallas_call(
        _mem_kv_kernel,
        grid=(n // tm,),
        in_specs=[pl.BlockSpec((tm, d), lambda i: (i, 0)), _const_spec((1, d)),
                  _const_spec(w_kv.shape)],
        out_specs=(pl.BlockSpec((tm, half), lambda i: (i, 0)),) * 2,
        out_shape=(jax.ShapeDtypeStruct((n, half), F32),) * 2,
        compiler_params=_cparams(1),
        name="mem_kv",
    )(mem2d, gain.reshape(1, d), w_kv)


def _attend(q, k, v, scale):
    s = _mm_nt(q, k) * scale
    e = jnp.exp(s - jnp.max(s, axis=-1, keepdims=True))
    return _mm(e, v) / jnp.sum(e, axis=-1, keepdims=True)


def _mem_attn_sample_kernel(q_ref, k_ref, v_ref, o_ref, *, nseq, rows, heads, hd):
    scale = hd ** -0.5
    mh = k_ref.shape[1]
    same_head = ((lax.broadcasted_iota(jnp.int32, (rows, mh), 0) & (heads - 1))
                 == (lax.broadcasted_iota(jnp.int32, (rows, mh), 1) & (heads - 1)))
    for b in range(nseq):
        rs = slice(b * rows, (b + 1) * rows)
        s = jnp.where(same_head, _mm_nt(q_ref[rs, :], k_ref[b]) * scale, -jnp.inf)
        e = jnp.exp(s - jnp.max(s, axis=-1, keepdims=True))
        o_ref[rs, :] = _mm(e, v_ref[b]) / jnp.sum(e, axis=-1, keepdims=True)


def _mem_attn_sample(q_rows, mk, mv, heads, rows, nseq):
    n, hd = q_rows.shape
    b, mh, _ = mk.shape
    assert heads & (heads - 1) == 0
    kern = functools.partial(_mem_attn_sample_kernel, nseq=nseq, rows=rows, heads=heads, hd=hd)
    return pl.pallas_call(
        kern,
        grid=(b // nseq,),
        in_specs=[pl.BlockSpec((nseq * rows, hd), lambda i: (i, 0)),
                  pl.BlockSpec((nseq, mh, hd), lambda i: (i, 0, 0)),
                  pl.BlockSpec((nseq, mh, hd), lambda i: (i, 0, 0))],
        out_specs=pl.BlockSpec((nseq * rows, hd), lambda i: (i, 0)),
        out_shape=jax.ShapeDtypeStruct((n, hd), F32),
        compiler_params=_cparams(1),
        name="mem_attn_sample",
    )(q_rows, mk, mv)


def _merge_kernel(*refs, mem_heads):
    if mem_heads:
        (x_ref, odn_ref, osg_ref, qm_ref, mk_ref, mv_ref, g_ref, wg_ref, wdn_ref, wsg_ref, wmem_ref,
         wo_ref, o_ref) = refs
        hd = qm_ref.shape[1] // mem_heads
        omem = jnp.concatenate(
            [_attend(qm_ref[:, h * hd:(h + 1) * hd], mk_ref[0, :, h * hd:(h + 1) * hd],
                     mv_ref[0, :, h * hd:(h + 1) * hd], hd ** -0.5) for h in range(mem_heads)], axis=1)
    else:
        (x_ref, odn_ref, osg_ref, omem_ref, g_ref, wg_ref, wdn_ref, wsg_ref, wmem_ref,
         wo_ref, o_ref) = refs
        omem = omem_ref[...]
    x = x_ref[...]
    d = x.shape[-1]
    xn = _rms(x, g_ref[...]).astype(BF16)
    merged = None
    for idx, (br, w_ref) in enumerate(((odn_ref[...], wdn_ref), (osg_ref[...], wsg_ref), (omem, wmem_ref))):
        gate = _sigmoid(_mm_nt(xn, wg_ref[idx * d:(idx + 1) * d, :]))
        term = gate * _mm(br, w_ref[...])
        merged = term if merged is None else merged + term
    o_ref[...] = x + _mm(merged, wo_ref[...])


def _merge(x2d, odn, osg, mem, norm_mix, w_t, gate_row0, wdn, wsg, wmem, wo, tm, mem_heads=0,
           seq_len=None):
    n, d = x2d.shape
    row = lambda a: pl.BlockSpec((tm, a.shape[1]), lambda i: (i, 0))
    if mem_heads:
        qm, mk, mv = mem
        seq_tiles = seq_len // tm
        kvspec = pl.BlockSpec((1,) + mk.shape[1:], lambda i: (i // seq_tiles, 0, 0))
        mem_specs, mem_args = [row(qm), kvspec, kvspec], [qm, mk, mv]
    else:
        mem_specs, mem_args = [row(mem)], [mem]
    return pl.pallas_call(
        functools.partial(_merge_kernel, mem_heads=mem_heads),
        grid=(n // tm,),
        in_specs=[row(x2d), row(odn), row(osg)] + mem_specs + [
                  _const_spec((1, d)),
                  _row_window_spec(gate_row0, 3 * d, d), _const_spec(wdn.shape), _const_spec(wsg.shape),
                  _const_spec(wmem.shape), _const_spec(wo.shape)],
        out_specs=row(x2d),
        out_shape=jax.ShapeDtypeStruct((n, d), F32),
        compiler_params=_cparams(1),
        name="merge_attn" if mem_heads else "merge",
    )(x2d, odn, osg, *mem_args, norm_mix.reshape(1, d), w_t, wdn, wsg, wmem, wo)


def _ffn_kernel(x_ref, gf_ref, wgu_ref, wd_ref, gl_ref, o_ref):
    x = x_ref[...]
    dff = wd_ref.shape[0]
    hn = _rms(x, gf_ref[...]).astype(BF16)
    hg = jnp.dot(hn, wgu_ref[:, :dff], preferred_element_type=F32)
    hu = jnp.dot(hn, wgu_ref[:, dff:], preferred_element_type=F32)
    x2 = x + _mm(_silu(hg) * hu, wd_ref[...])
    o_ref[...] = _rms(x2, gl_ref[...])


def _ffn(x2d, norm_ffn, wgu, wd, norm_final, tm):
    n, d = x2d.shape
    return pl.pallas_call(
        _ffn_kernel,
        grid=(n // tm,),
        in_specs=[pl.BlockSpec((tm, d), lambda i: (i, 0)), _const_spec((1, d)),
                  _const_spec(wgu.shape), _const_spec(wd.shape), _const_spec((1, d))],
        out_specs=pl.BlockSpec((tm, d), lambda i: (i, 0)),
        out_shape=jax.ShapeDtypeStruct((n, d), F32),
        compiler_params=_cparams(1),
        name="ffn",
    )(x2d, norm_ffn.reshape(1, d), wgu, wd, norm_final.reshape(1, d))


def kernel(x_prompt, x_sample, state_delta, state_conv, cache_mem_k, cache_mem_v, mem_prompt,
           norm_mix, w_in, conv_w, a_log, dt_bias, dn_norm, sgu_norm, sgu_w, sgu_b,
           w_br_dn, w_br_sgu, w_br_mem, w_o, mem_norm, w_mem_kv, norm_ffn, w_gate_up, w_down,
           norm_final):
    depth = w_in.shape[0]
    assert depth == 1, "single-layer stack only"
    bp, tp, d = x_prompt.shape
    bs, ts, _ = x_sample.shape
    heads, dk, dv = state_delta.shape[2:]
    assert dk == dv == LANES and 2 * heads <= LANES and ts == SUBLANES
    kw = heads * dk
    qkv_w = 2 * kw + heads * dv
    sgu_groups, sgu_chunk, _ = sgu_w.shape[1:]
    sgu_width = sgu_norm.shape[-1]
    mem_tokens, mem_heads, mem_hd = cache_mem_k.shape[2:]
    mem_w = mem_heads * mem_hd
    l = 0

    splits = (qkv_w, heads * dv, heads, heads, sgu_width, sgu_width, mem_w, 3 * d)
    o = [0]
    for s in splits:
        o.append(o[-1] + s)
    assert all(o[k] % 16 == 0 for k in (1, 2, 4, 5, 6, 7))
    w_t = w_in[l].T.astype(BF16)
    widths = (qkv_w, heads * dv, sgu_width, sgu_width, mem_w, LANES)

    lane_pad = lambda vec: jnp.zeros((1, LANES), F32).at[0, heads:2 * heads].set(vec)
    alog_row = lane_pad(a_log[l])
    dtb_row = lane_pad(dt_bias[l])

    wdn = w_br_dn[l].astype(BF16)
    wsg = w_br_sgu[l].astype(BF16)
    wmem = w_br_mem[l].astype(BF16)
    wo = w_o[l].astype(BF16)
    wgu = w_gate_up[l].astype(BF16)
    wd = w_down[l].astype(BF16)
    wkv = w_mem_kv[l].astype(BF16)

    tm = 512

    np_ = bp * tp
    xp2 = x_prompt.reshape(np_, d)
    qkv_p, zg_p, ba_p, osg_p, qm_p, tail_p = _in_proj(
        xp2, norm_mix[l], w_t, sgu_norm[l], sgu_w[l], sgu_b[l].T, SGU_CHUNK, widths, tm, heads, dk,
        conv_w=conv_w[l], seq_len=tp)
    odn_p, s_p = _delta_prompt(qkv_p.reshape(bp, tp, qkv_w), zg_p.reshape(bp, tp, -1),
                               ba_p.reshape(bp, tp, LANES), alog_row, dtb_row,
                               dn_norm[l], heads, dk, 256)
    conv_p = tail_p[:, SUBLANES - (CONV_W - 1):, :]
    mk_p, mv_p = _mem_kv(mem_prompt.reshape(bp * mem_tokens, d), mem_norm[l], wkv, tm)
    mk_p = mk_p.reshape(bp, mem_tokens, mem_w)
    mv_p = mv_p.reshape(bp, mem_tokens, mem_w)
    x1_p = _merge(xp2, odn_p.reshape(np_, -1), osg_p, (qm_p, mk_p, mv_p), norm_mix[l],
                  w_t, o[7], wdn, wsg, wmem, wo, tm, mem_heads=mem_heads, seq_len=tp)
    y_p = _ffn(x1_p, norm_ffn[l], wgu, wd, norm_final, tm)

    ns = bs * ts
    xs2 = x_sample.reshape(ns, d)
    reps = SGU_CHUNK // ts
    w_tiles = jnp.tile(sgu_w[l][:, :ts, :ts], (1, reps, reps))
    b_cols = jnp.tile(sgu_b[l][:, :ts], (1, reps)).T
    qkv_s, zg_s, ba_s, osg_s, qm_s, v_s = _in_proj(
        xs2, norm_mix[l], w_t, sgu_norm[l], w_tiles, b_cols, ts, widths, tm, heads, dk)
    qkv_s3 = qkv_s.reshape(bs, ts, qkv_w)
    hist = jnp.concatenate(
        [jnp.zeros((bs, ts - (CONV_W - 1), qkv_w), F32), state_conv[l]], axis=1).reshape(ns, qkv_w)
    nseq = LANES // ts
    odn_s, s_s = _delta_sample(hist, qkv_s, zg_s, ba_s, conv_w[l], alog_row, dtb_row, dn_norm[l],
                               state_delta[l], heads, dk, ts, nseq, 4)
    conv_s = qkv_s3[:, ts - (CONV_W - 1):, :]
    omem_s = _mem_attn_sample(qm_s.reshape(ns * mem_heads, mem_hd),
                              cache_mem_k[l].reshape(bs, mem_tokens * mem_heads, mem_hd),
                              cache_mem_v[l].reshape(bs, mem_tokens * mem_heads, mem_hd),
                              mem_heads, ts * mem_heads, 8).reshape(ns, mem_w)
    x1_s = _merge(xs2, odn_s, osg_s, omem_s, norm_mix[l], w_t, o[7], wdn, wsg, wmem, wo, tm)
    y_s = _ffn(x1_s, norm_ffn[l], wgu, wd, norm_final, tm)

    return (y_p.reshape(bp, tp, d), y_s.reshape(bs, ts, d),
            s_p[None], conv_p[None],
            mk_p.reshape(1, bp, mem_tokens, mem_heads, mem_hd),
            mv_p.reshape(1, bp, mem_tokens, mem_heads, mem_hd),
            s_s[None], conv_s[None], v_s.reshape(1, bs, ts, sgu_width))
```
